```python
import math
import jax
import jax.numpy as jnp
from jax import lax
import numpy as np

D_MODEL = 1024
BATCH = 4
SEQ = 8192
DEPTH = 1
DEC_BATCH = 32
DEC_SEQ = 1
PAST_LEN = 16384
PAGE_SIZE = 128

HG_HEADS = 8
HG_DK = 128
HG_DV = D_MODEL // HG_HEADS
HG_WIDTH_K = HG_HEADS * HG_DK
HG_WIDTH_V = HG_HEADS * HG_DV
HG_CHUNK = 64
SB_HEADS = 8
SB_DIM = 128
SB_WIDTH = SB_HEADS * SB_DIM
SB_BLOCK = 128
SB_BIAS_INIT = -8.0
N_IN = 2 * HG_WIDTH_K + 2 * HG_WIDTH_V + 3 * SB_WIDTH + 2 * D_MODEL
PEER_HEADS = 8
PEER_N_KEYS = 128
PEER_N_EXPERTS = PEER_N_KEYS * PEER_N_KEYS
PEER_QDIM = 256
PEER_HALF = PEER_QDIM // 2
PEER_TOPK = 16
PEER_TOKEN_BLOCK = 128
RMS_EPS = 1e-6

kernel_name = "hgrn2_stickbreaking_peer_step"


def split_points():
    widths = (HG_WIDTH_K, HG_WIDTH_K, HG_WIDTH_V, HG_WIDTH_V, SB_WIDTH, SB_WIDTH, SB_WIDTH, D_MODEL, D_MODEL)
    return tuple(int(c) for c in np.cumsum(widths)[:-1])


def rmsnorm(x, g):
    xf = x.astype(jnp.float32)
    y = xf * lax.rsqrt(jnp.mean(xf * xf, axis=-1, keepdims=True) + RMS_EPS)
    return (y * g.astype(jnp.float32)).astype(x.dtype)


def hgrn2_scan(q, k, v, logf, s0, chunk):
    b_, t_, h_, _ = q.shape
    dv = v.shape[-1]
    nc = t_ // chunk

    def to_chunks(a):
        return a.reshape(b_, nc, chunk, h_, a.shape[-1]).transpose(1, 0, 3, 2, 4)

    causal = jnp.tril(jnp.ones((chunk, chunk), dtype=bool))[:, :, None]

    def step(s_prev, inp):
        qc, kc, vc, gc = inp
        cum = jnp.cumsum(gc, axis=2)
        o_inter = jnp.einsum('bhtc,bhcv->bhtv', qc * jnp.exp(cum), s_prev)
        rel = cum[:, :, :, None, :] - cum[:, :, None, :, :]
        decay = jnp.where(causal, jnp.exp(jnp.minimum(rel, 0.0)), 0.0)
        scores = jnp.einsum('bhtc,bhsc,bhtsc->bhts', qc, kc, decay)
        o = o_inter + jnp.einsum('bhts,bhsv->bhtv', scores, vc)
        last = cum[:, :, -1, :]
        s_new = jnp.exp(last)[..., None] * s_prev + jnp.einsum(
            'bhsc,bhsv->bhcv', kc * jnp.exp(last[:, :, None, :] - cum), vc)
        return s_new, o

    s_fin, o = lax.scan(step, s0, (to_chunks(q), to_chunks(k), to_chunks(v), to_chunks(logf)))
    o = o.transpose(1, 0, 3, 2, 4).reshape(b_, t_, h_, dv)
    return o, s_fin


def hgrn2_branch(q_raw, f_raw, i_raw, g_raw, lb, norm_g, s0):
    b_, t_, _ = q_raw.shape
    chunk = math.gcd(t_, HG_CHUNK)
    f = lb + (1.0 - lb) * jax.nn.sigmoid(f_raw.astype(jnp.float32))
    logf = jnp.log(f)
    k = 1.0 - f
    q = jax.nn.silu(q_raw.astype(jnp.float32))
    o, s_fin = hgrn2_scan(
        q.reshape(b_, t_, HG_HEADS, HG_DK),
        k.reshape(b_, t_, HG_HEADS, HG_DK),
        i_raw.astype(jnp.float32).reshape(b_, t_, HG_HEADS, HG_DV),
        logf.reshape(b_, t_, HG_HEADS, HG_DK),
        s0.astype(jnp.float32), chunk)
    o = o * lax.rsqrt(jnp.mean(o * o, axis=-1, keepdims=True) + RMS_EPS)
    o = o.reshape(b_, t_, HG_WIDTH_V) * norm_g.astype(jnp.float32) * jax.nn.silu(g_raw.astype(jnp.float32))
    return o.astype(q_raw.dtype), s_fin


def stick_breaking(q, k, v, q_start, bias):
    b_, tq, h_, d = q.shape
    tk = k.shape[1]
    qb = math.gcd(tq, SB_BLOCK)
    nb = tq // qb
    scale = d ** -0.5
    kpos = jnp.arange(tk)
    bias_f = bias.astype(jnp.float32)[None, :, None, None]
    q_blocks = q.reshape(b_, nb, qb, h_, d).transpose(1, 0, 2, 3, 4)

    def one_block(args):
        q_blk, bi = args
        qpos = q_start + bi * qb + jnp.arange(qb)
        z = jnp.einsum('bqhd,bkhd->bhqk', q_blk, k).astype(jnp.float32) * scale + bias_f
        visible = (kpos[None, :] < qpos[:, None])[None, None]
        log_keep = jnp.where(visible, jax.nn.log_sigmoid(-z), 0.0)
        later = lax.cumsum(log_keep, axis=3, reverse=True) - log_keep
        weights = jnp.where(visible, jnp.exp(jax.nn.log_sigmoid(z) + later), 0.0)
        return jnp.einsum('bhqk,bkhd->bqhd', weights.astype(v.dtype), v)

    out = lax.map(one_block, (q_blocks, jnp.arange(nb)))
    return out.transpose(1, 0, 2, 3, 4).reshape(b_, tq, h_ * d).astype(q.dtype)


def mixer_block(x, k_past, v_past, s0, lb, g_attn, w_in, hg_norm, sb_bias, w_branch_a, w_branch_b, w_out):
    b_, t_, _ = x.shape
    h = rmsnorm(x, g_attn)
    proj = h @ w_in
    qa, fa, ia, ga, qs, ks, vs, gate_a, gate_b = jnp.split(proj, split_points(), axis=-1)
    o_a, s_new = hgrn2_branch(qa, fa, ia, ga, lb, hg_norm, s0)
    q_sb = qs.reshape(b_, t_, SB_HEADS, SB_DIM)
    k_new = ks.reshape(b_, t_, SB_HEADS, SB_DIM)
    v_new = vs.reshape(b_, t_, SB_HEADS, SB_DIM)
    if k_past is None:
        k_all, v_all, q_start = k_new, v_new, 0
    else:
        k_all = jnp.concatenate([k_past.astype(k_new.dtype), k_new], axis=1)
        v_all = jnp.concatenate([v_past.astype(v_new.dtype), v_new], axis=1)
        q_start = k_past.shape[1]
    o_b = stick_breaking(q_sb, k_all, v_all, q_start, sb_bias)
    merged = jax.nn.sigmoid(gate_a) * (o_a @ w_branch_a) + jax.nn.sigmoid(gate_b) * (o_b @ w_branch_b)
    return x + merged @ w_out, s_new, k_new, v_new


def peer(h, w_q, sub_keys, u_tab, v_tab):
    b_, t_, d = h.shape
    n = b_ * t_
    blk = math.gcd(n, PEER_TOKEN_BLOCK)
    nb = n // blk

    def one_block(xb):
        q = (xb @ w_q).reshape(blk, PEER_HEADS, 2, PEER_HALF).astype(jnp.float32)
        s = jnp.einsum('thpc,pkc->thpk', q, sub_keys.astype(jnp.float32))
        s_top, i_top = lax.top_k(s, PEER_TOPK)
        cand = (s_top[:, :, 0, :, None] + s_top[:, :, 1, None, :]).reshape(blk, PEER_HEADS, PEER_TOPK * PEER_TOPK)
        cand_idx = (i_top[:, :, 0, :, None] * PEER_N_KEYS + i_top[:, :, 1, None, :]).reshape(
            blk, PEER_HEADS, PEER_TOPK * PEER_TOPK)
        best, pos = lax.top_k(cand, PEER_TOPK)
        experts = jnp.take_along_axis(cand_idx, pos, axis=-1)
        gates = jax.nn.softmax(best, axis=-1)
        u = u_tab[experts]
        act = jax.nn.gelu(jnp.einsum('td,thkd->thk', xb, u).astype(jnp.float32), approximate=False)
        w = (gates * act).astype(xb.dtype)
        return jnp.einsum('thk,thkd->td', w, v_tab[experts])

    out = lax.map(one_block, h.reshape(nb, blk, d))
    return out.reshape(b_, t_, d).astype(h.dtype)


def setup_inputs(seed: int = 0) -> dict:
    key = jax.random.key(seed)
    ks = jax.random.split(key, 20)
    n_pages = PAST_LEN // PAGE_SIZE
    n_used = DEC_BATCH * n_pages
    n_pool = n_used + n_used // 4

    def normal(k, shape, scale):
        return scale * jax.random.normal(k, shape, jnp.float32)

    return {
        "x_prompt": normal(ks[0], (BATCH, SEQ, D_MODEL), 1.0),
        "x_sample": normal(ks[1], (DEC_BATCH, DEC_SEQ, D_MODEL), 1.0),
        "cache_k": normal(ks[2], (DEPTH, n_pool, PAGE_SIZE, SB_HEADS, SB_DIM), 1.0),
        "cache_v": normal(ks[3], (DEPTH, n_pool, PAGE_SIZE, SB_HEADS, SB_DIM), 1.0),
        "state_hgrn": normal(ks[4], (DEPTH, DEC_BATCH, HG_HEADS, HG_DK, HG_DV), 0.5),
        "page_table": jax.random.permutation(ks[5], n_pool)[:n_used].reshape(DEC_BATCH, n_pages).astype(jnp.int32),
        "g_attn": 1.0 + normal(ks[6], (DEPTH, D_MODEL), 0.01),
        "w_in": normal(ks[7], (DEPTH, D_MODEL, N_IN), D_MODEL ** -0.5),
        "hg_lb_logits": normal(ks[8], (DEPTH + 1, HG_WIDTH_K), 0.1),
        "hg_norm": 1.0 + normal(ks[9], (DEPTH, HG_WIDTH_V), 0.01),
        "sb_bias": SB_BIAS_INIT + normal(ks[19], (DEPTH, SB_HEADS), 0.5),
        "w_branch_a": normal(ks[10], (DEPTH, HG_WIDTH_V, D_MODEL), HG_WIDTH_V ** -0.5),
        "w_branch_b": normal(ks[11], (DEPTH, SB_WIDTH, D_MODEL), SB_WIDTH ** -0.5),
        "w_out": normal(ks[12], (DEPTH, D_MODEL, D_MODEL), D_MODEL ** -0.5),
        "g_ffn": 1.0 + normal(ks[13], (DEPTH, D_MODEL), 0.01),
        "peer_wq": normal(ks[14], (DEPTH, D_MODEL, PEER_HEADS * PEER_QDIM), D_MODEL ** -0.5),
        "peer_subkeys": normal(ks[15], (DEPTH, 2, PEER_N_KEYS, PEER_HALF), PEER_HALF ** -0.5),
        "peer_u": normal(ks[16], (DEPTH, PEER_N_EXPERTS, D_MODEL), D_MODEL ** -0.5),
        "peer_v": normal(ks[17], (DEPTH, PEER_N_EXPERTS, D_MODEL), PEER_HEADS ** -0.5),
        "g_final": 1.0 + normal(ks[18], (D_MODEL,), 0.01),
    }


def reference(x_prompt, x_sample, cache_k, cache_v, state_hgrn, page_table, g_attn, w_in, hg_lb_logits,
              hg_norm, sb_bias, w_branch_a, w_branch_b, w_out, g_ffn, peer_wq, peer_subkeys, peer_u, peer_v,
              g_final):
    lb_all = jnp.cumsum(jax.nn.softmax(hg_lb_logits.astype(jnp.float32), axis=0), axis=0)
    dec_b, n_pages = page_table.shape
    page_size = cache_k.shape[2]
    s0_prompt = jnp.zeros((x_prompt.shape[0], HG_HEADS, HG_DK, HG_DV), jnp.float32)
    xp, xs = x_prompt, x_sample
    sp_list, kp_list, vp_list, ss_list, ks_list, vs_list = [], [], [], [], [], []
    for l in range(DEPTH):
        k_past = cache_k[l][page_table].reshape(dec_b, n_pages * page_size, SB_HEADS, SB_DIM)
        v_past = cache_v[l][page_table].reshape(dec_b, n_pages * page_size, SB_HEADS, SB_DIM)
        xp, sp, kp, vp = mixer_block(xp, None, None, s0_prompt, lb_all[l], g_attn[l], w_in[l], hg_norm[l],
                                     sb_bias[l], w_branch_a[l], w_branch_b[l], w_out[l])
        xs, ss, ksn, vsn = mixer_block(xs, k_past, v_past, state_hgrn[l], lb_all[l], g_attn[l], w_in[l],
                                       hg_norm[l], sb_bias[l], w_branch_a[l], w_branch_b[l], w_out[l])
        xp = xp + peer(rmsnorm(xp, g_ffn[l]), peer_wq[l], peer_subkeys[l], peer_u[l], peer_v[l])
        xs = xs + peer(rmsnorm(xs, g_ffn[l]), peer_wq[l], peer_subkeys[l], peer_u[l], peer_v[l])
        sp_list.append(sp.astype(state_hgrn.dtype))
        kp_list.append(kp)
        vp_list.append(vp)
        ss_list.append(ss.astype(state_hgrn.dtype))
        ks_list.append(ksn)
        vs_list.append(vsn)
    y_prompt = rmsnorm(xp, g_final)
    y_sample = rmsnorm(xs, g_final)
    return (y_prompt, y_sample, jnp.stack(sp_list), jnp.stack(kp_list), jnp.stack(vp_list),
            jnp.stack(ss_list), jnp.stack(ks_list), jnp.stack(vs_list))
```

```python
import functools

import numpy as np
import jax
import jax.numpy as jnp
from jax import lax
from jax.experimental import pallas as pl
from jax.experimental.pallas import tpu as pltpu

F32 = jnp.float32
BF16 = jnp.bfloat16

D_MODEL = 1024
N_HEADS = 8
HEAD_DIM = 128
N_GROUPS = 9
HG_CHUNK = 64
SB_BLOCK = 128
PEER_KEYS = 128
PEER_TOPK = 16
PEER_SEL = N_HEADS * PEER_TOPK
RMS_EPS = 1e-6
NEG_INF = float("-inf")

G_HQ, G_HF, G_HI, G_HG, G_SQ, G_SK, G_SV, G_GA, G_GB = range(N_GROUPS)


def _dot(a, b):
    return jnp.dot(a, b, preferred_element_type=F32)


def _dot_nt(a, b):
    return lax.dot_general(a, b, (((1,), (1,)), ((), ())), preferred_element_type=F32)


def _dot_tn(a, b):
    return lax.dot_general(a, b, (((0,), (0,)), ((), ())), preferred_element_type=F32)


def _split2(x):
    hi = x.astype(BF16)
    lo = (x - hi.astype(F32)).astype(BF16)
    return hi, lo


def _split3(x):
    hi = x.astype(BF16)
    r = x - hi.astype(F32)
    mid = r.astype(BF16)
    lo = (r - mid.astype(F32)).astype(BF16)
    return hi, mid, lo


def _exact_dot_lhs01(m01, x):
    hi, mid, lo = _split3(x)
    return _dot(m01, hi) + _dot(m01, mid) + _dot(m01, lo)


def _exact_dot_rhs01(x, m01):
    hi, mid, lo = _split3(x)
    return _dot(hi, m01) + _dot(mid, m01) + _dot(lo, m01)


def _rms(x, g):
    return x * lax.rsqrt(jnp.mean(x * x, axis=-1, keepdims=True) + RMS_EPS) * g


def _silu(x):
    return x * jax.nn.sigmoid(x)


def _norm_proj_kernel(x_ref, g_ref, w_ref, o_ref, h_ref):
    @pl.when(pl.program_id(1) == 0)
    def _():
        h_ref[...] = _rms(x_ref[...], g_ref[...]).astype(BF16)

    o_ref[0] = _dot(h_ref[...], w_ref[...])


def _norm_proj(x, g, w, tm):
    n = x.shape[0]
    return pl.pallas_call(
        _norm_proj_kernel,
        grid=(n // tm, N_GROUPS),
        in_specs=[
            pl.BlockSpec((tm, D_MODEL), lambda i, j: (i, 0)),
            pl.BlockSpec((1, D_MODEL), lambda i, j: (0, 0)),
            pl.BlockSpec((D_MODEL, D_MODEL), lambda i, j: (0, j)),
        ],
        out_specs=pl.BlockSpec((1, tm, D_MODEL), lambda i, j: (j, i, 0)),
        out_shape=jax.ShapeDtypeStruct((N_GROUPS, n, D_MODEL), F32),
        scratch_shapes=[pltpu.VMEM((tm, D_MODEL), BF16)],
        compiler_params=pltpu.CompilerParams(dimension_semantics=("parallel", "arbitrary")),
        name="norm_proj",
    )(x, g, w)


def _hgrn_consts():
    c = HG_CHUNK
    t = np.arange(c)[:, None]
    j = np.arange(c)[None, :]
    sel = [(j <= t)]
    masks = []
    m = c // 2
    while m >= 1:
        mid = (t // (2 * m)) * (2 * m) + m
        upper = t >= mid
        d = np.where(upper, (j >= mid) & (j <= t), (j > t) & (j <= mid - 1))
        sel.append(d)
        s = j
        same = (t // (2 * m)) == (s // (2 * m))
        s_mid = (s // (2 * m)) * (2 * m) + m
        masks.append(same & upper & (s < s_mid))
        m //= 2
    sel.append(j > t)
    masks.append(t == j)
    sel = np.concatenate(sel, axis=0).astype(np.float32)
    masks = np.stack(masks).astype(np.float32)
    return jnp.asarray(sel, dtype=BF16), jnp.asarray(masks, dtype=F32)


def _lower_bound(lbl, layer):
    e = jnp.exp(lbl - jnp.max(lbl, axis=0, keepdims=True))
    return jnp.sum(e[: layer + 1], axis=0, keepdims=True) / jnp.sum(e, axis=0, keepdims=True)


def _hgrn_kernel(q_ref, f_ref, i_ref, g_ref, lbl_ref, ng_ref, sel_ref, mask_ref,
                 o_ref, s_ref, st_ref, *, n_sub, layer):
    c = pl.program_id(2)
    n_lvl = mask_ref.shape[0] - 1
    ch = HG_CHUNK

    @pl.when(c == 0)
    def _():
        st_ref[...] = jnp.zeros_like(st_ref)

    lb = _lower_bound(lbl_ref[...], layer)
    ng = ng_ref[...]
    sel = sel_ref[...]

    def sub(sc, carry):
        rows = pl.ds(pl.multiple_of(sc * ch, ch), ch)
        f = lb + (1.0 - lb) * jax.nn.sigmoid(f_ref[0, rows, :])
        logf = jnp.log(f)
        k = 1.0 - f
        q = _silu(q_ref[0, rows, :])
        v = i_ref[0, rows, :].astype(BF16)
        ex = jnp.exp(_exact_dot_lhs01(sel, logf))
        st = st_ref[...]
        o = _dot_nt((q * ex[0:ch]).astype(BF16), st.astype(BF16))
        scores = _dot_nt(q.astype(BF16), k.astype(BF16)) * mask_ref[n_lvl]
        for li in range(n_lvl):
            el = ex[(li + 1) * ch:(li + 2) * ch]
            scores += _dot_nt((q * el).astype(BF16), (k * el).astype(BF16)) * mask_ref[li]
        o += _dot(scores.astype(BF16), v)
        e_last = ex[(n_lvl + 1) * ch:(n_lvl + 2) * ch]
        st_ref[...] = st * ex[ch - 1:ch] + _dot_tn(v, (k * e_last).astype(BF16))
        o = o * lax.rsqrt(jnp.mean(o * o, axis=-1, keepdims=True) + RMS_EPS)
        o_ref[rows, :] = (o * ng * _silu(g_ref[0, rows, :])).astype(o_ref.dtype)
        return carry

    lax.fori_loop(0, n_sub, sub, 0)

    @pl.when(c == pl.num_programs(2) - 1)
    def _():
        s_ref[0, 0] = st_ref[...].T


def _hgrn_prompt(proj, lb_logits, norm_g, batch, seq, layer, tc):
    sel, masks = _hgrn_consts()
    nc = seq // tc
    n = batch * seq

    def pmap(g):
        return lambda b, h, c: (g, b * nc + c, h)

    kern = functools.partial(_hgrn_kernel, n_sub=tc // HG_CHUNK, layer=layer)
    return pl.pallas_call(
        kern,
        grid=(batch, N_HEADS, nc),
        in_specs=[
            pl.BlockSpec((1, tc, HEAD_DIM), pmap(G_HQ)),
            pl.BlockSpec((1, tc, HEAD_DIM), pmap(G_HF)),
            pl.BlockSpec((1, tc, HEAD_DIM), pmap(G_HI)),
            pl.BlockSpec((1, tc, HEAD_DIM), pmap(G_HG)),
            pl.BlockSpec((lb_logits.shape[0], HEAD_DIM), lambda b, h, c: (0, h)),
            pl.BlockSpec((1, HEAD_DIM), lambda b, h, c: (0, h)),
            pl.BlockSpec(sel.shape, lambda b, h, c: (0, 0)),
            pl.BlockSpec(masks.shape, lambda b, h, c: (0, 0, 0)),
        ],
        out_specs=[
            pl.BlockSpec((tc, HEAD_DIM), lambda b, h, c: (b * nc + c, h)),
            pl.BlockSpec((1, 1, HEAD_DIM, HEAD_DIM), lambda b, h, c: (b, h, 0, 0)),
        ],
        out_shape=[
            jax.ShapeDtypeStruct((n, D_MODEL), BF16),
            jax.ShapeDtypeStruct((batch, N_HEADS, HEAD_DIM, HEAD_DIM), F32),
        ],
        scratch_shapes=[pltpu.VMEM((HEAD_DIM, HEAD_DIM), F32)],
        compiler_params=pltpu.CompilerParams(
            dimension_semantics=("parallel", "parallel", "arbitrary")),
        name="hgrn_prompt",
    )(proj, proj, proj, proj, lb_logits, norm_g, sel, masks)


def _hgrn_step_kernel(q_ref, f_ref, i_ref, g_ref, lbl_ref, ng_ref, s_ref, o_ref, sn_ref, *, layer):
    lb = _lower_bound(lbl_ref[...], layer)
    lb = lb[0]
    f = lb + (1.0 - lb) * jax.nn.sigmoid(f_ref[0, 0])
    q = _silu(q_ref[0, 0])
    v = i_ref[0, 0]
    g = g_ref[0, 0]
    ng = ng_ref[...]
    pad = jnp.zeros((HEAD_DIM - 2 * N_HEADS, HEAD_DIM), F32)
    cols = jnp.concatenate([f, q, pad], axis=0).T
    for h in range(N_HEADS):
        fcol = cols[:, h:h + 1]
        qcol = cols[:, N_HEADS + h:N_HEADS + h + 1]
        s_new = fcol * s_ref[0, h] + (1.0 - fcol) * v[h:h + 1, :]
        sn_ref[0, h] = s_new
        o = jnp.sum(qcol * s_new, axis=0, keepdims=True)
        o = o * lax.rsqrt(jnp.mean(o * o, axis=-1, keepdims=True) + RMS_EPS)
        o_ref[0, pl.ds(h, 1), :] = o * ng[h:h + 1, :] * _silu(g[h:h + 1, :])


def _hgrn_step(proj, lb_logits, norm_g, state, layer):
    nb = state.shape[0]
    proj4 = proj.reshape(N_GROUPS, nb, N_HEADS, HEAD_DIM)
    lbl = lb_logits.reshape(lb_logits.shape[0], N_HEADS, HEAD_DIM)
    ng = norm_g.reshape(N_HEADS, HEAD_DIM)

    def pmap(g):
        return lambda b: (g, b, 0, 0)

    blk = (1, 1, N_HEADS, HEAD_DIM)
    return pl.pallas_call(
        functools.partial(_hgrn_step_kernel, layer=layer),
        grid=(nb,),
        in_specs=[
            pl.BlockSpec(blk, pmap(G_HQ)),
            pl.BlockSpec(blk, pmap(G_HF)),
            pl.BlockSpec(blk, pmap(G_HI)),
            pl.BlockSpec(blk, pmap(G_HG)),
            pl.BlockSpec(lbl.shape, lambda b: (0, 0, 0)),
            pl.BlockSpec(ng.shape, lambda b: (0, 0)),
            pl.BlockSpec((1, N_HEADS, HEAD_DIM, HEAD_DIM), lambda b: (b, 0, 0, 0)),
        ],
        out_specs=[
            pl.BlockSpec((1, N_HEADS, HEAD_DIM), lambda b: (b, 0, 0)),
            pl.BlockSpec((1, N_HEADS, HEAD_DIM, HEAD_DIM), lambda b: (b, 0, 0, 0)),
        ],
        out_shape=[
            jax.ShapeDtypeStruct((nb, N_HEADS, HEAD_DIM), F32),
            jax.ShapeDtypeStruct(state.shape, F32),
        ],
        name="hgrn_step",
    )(proj4, proj4, proj4, proj4, lbl, ng, state)


def _later_sum_consts():
    j = np.arange(SB_BLOCK)[:, None]
    k = np.arange(SB_BLOCK)[None, :]
    u = np.concatenate([(j > k), np.ones((SB_BLOCK, SB_BLOCK), bool)], axis=1)
    return jnp.asarray(u.astype(np.float32), dtype=BF16)


def _log_beta_terms(z):
    l1p = jnp.log1p(jnp.exp(-jnp.abs(z)))
    return jnp.minimum(z, 0.0) - l1p, -jnp.maximum(z, 0.0) - l1p


def _sb_prompt_kernel(q_ref, k_ref, v_ref, bias_ref, uo_ref, o_ref):
    h = pl.program_id(1)
    qi = pl.program_id(2)
    blk = SB_BLOCK
    scale = HEAD_DIM ** -0.5
    q = q_ref[0].astype(BF16)
    bias = bias_ref[pl.ds(h, 1), :]
    uo = uo_ref[...]
    visible = (lax.broadcasted_iota(jnp.int32, (blk, blk), 1)
               < lax.broadcasted_iota(jnp.int32, (blk, blk), 0))

    def block(kb, carry, acc, diagonal):
        rows = pl.ds(pl.multiple_of(kb * blk, blk), blk)
        k = k_ref[0, rows, :].astype(BF16)
        v = v_ref[0, rows, :].astype(BF16)
        z = _dot_nt(q, k) * scale + bias
        log_b, log_keep = _log_beta_terms(z)
        if diagonal:
            log_keep = jnp.where(visible, log_keep, 0.0)
        hi, lo = _split2(log_keep)
        cs = _dot(hi, uo) + _dot(lo, uo)
        w = jnp.exp(log_b + cs[:, :blk] + carry)
        if diagonal:
            w = jnp.where(visible, w, 0.0)
        return carry + cs[:, blk:], acc + _dot(w.astype(BF16), v)

    zeros = jnp.zeros((blk, blk), F32)
    carry, acc = block(qi, zeros, zeros, True)

    def body(s, ca):
        return block(qi - 1 - s, ca[0], ca[1], False)

    carry, acc = lax.fori_loop(0, qi, body, (carry, acc))
    o_ref[...] = acc.astype(o_ref.dtype)


def _sb_prompt(proj, bias_b, batch, seq):
    nq = seq // SB_BLOCK
    n = batch * seq
    uo = _later_sum_consts()
    return pl.pallas_call(
        _sb_prompt_kernel,
        grid=(batch, N_HEADS, nq),
        in_specs=[
            pl.BlockSpec((1, SB_BLOCK, HEAD_DIM), lambda b, h, i: (G_SQ, b * nq + i, h)),
            pl.BlockSpec((1, seq, HEAD_DIM), lambda b, h, i: (G_SK, b, h)),
            pl.BlockSpec((1, seq, HEAD_DIM), lambda b, h, i: (G_SV, b, h)),
            pl.BlockSpec(bias_b.shape, lambda b, h, i: (0, 0)),
            pl.BlockSpec(uo.shape, lambda b, h, i: (0, 0)),
        ],
        out_specs=pl.BlockSpec((SB_BLOCK, HEAD_DIM), lambda b, h, i: (b * nq + i, h)),
        out_shape=jax.ShapeDtypeStruct((n, D_MODEL), BF16),
        compiler_params=pltpu.CompilerParams(
            dimension_semantics=("parallel", "parallel", "arbitrary")),
        name="sb_prompt",
    )(proj, proj, proj, bias_b, uo)


def _sb_decode_kernel(pt_ref, q_ref, bias_ref, uo_ref, *refs, pages_per_step):
    k_refs = refs[:pages_per_step]
    v_refs = refs[pages_per_step:2 * pages_per_step]
    o_ref, carry_ref, acc_ref = refs[2 * pages_per_step:]
    b = pl.program_id(0)
    s = pl.program_id(1)
    scale = HEAD_DIM ** -0.5
    own_head = (jnp.right_shift(lax.broadcasted_iota(jnp.int32, (N_HEADS, D_MODEL), 1), 7)
                == lax.broadcasted_iota(jnp.int32, (N_HEADS, D_MODEL), 0))

    @pl.when(s == 0)
    def _():
        carry_ref[...] = jnp.zeros_like(carry_ref)
        acc_ref[...] = jnp.zeros_like(acc_ref)

    qrow = q_ref[0, pl.ds(b, 1), :]
    q_bd = jnp.where(own_head, jnp.broadcast_to(qrow, (N_HEADS, D_MODEL)), 0.0).astype(BF16)
    bias = bias_ref[...]
    uo = uo_ref[...]
    carry = carry_ref[...]
    acc = acc_ref[...]
    for r in range(pages_per_step):
        z = _dot_nt(q_bd, k_refs[r][0].astype(BF16)) * scale + bias
        log_b, log_keep = _log_beta_terms(z)
        cs = _exact_dot_rhs01(log_keep, uo)
        w = jnp.exp(log_b + cs[:, :SB_BLOCK] + carry)
        acc = acc + _dot(w.astype(BF16), v_refs[r][0].astype(BF16))
        carry = carry + cs[:, SB_BLOCK:]
    carry_ref[...] = carry
    acc_ref[...] = acc

    @pl.when(s == pl.num_programs(1) - 1)
    def _():
        o_ref[0] = jnp.sum(jnp.where(own_head, acc, 0.0), axis=0, keepdims=True)


def _sb_decode(proj, bias_b, cache_k, cache_v, page_table, pages_per_step):
    nb, n_pages = page_table.shape
    n_pool, page = cache_k.shape[0], cache_k.shape[1]
    ck = cache_k.reshape(n_pool, page, D_MODEL)
    cv = cache_v.reshape(n_pool, page, D_MODEL)
    uo = _later_sum_consts()
    pt = page_table.reshape(-1)

    def page_map(r):
        return lambda b, s, pt: (pt[b * n_pages + n_pages - 1 - (s * pages_per_step + r)], 0, 0)

    def page_specs():
        return [pl.BlockSpec((1, page, D_MODEL), page_map(r)) for r in range(pages_per_step)]

    grid_spec = pltpu.PrefetchScalarGridSpec(
        num_scalar_prefetch=1,
        grid=(nb, n_pages // pages_per_step),
        in_specs=[
            pl.BlockSpec((1, nb, D_MODEL), lambda b, s, pt: (G_SQ, 0, 0)),
            pl.BlockSpec(bias_b.shape, lambda b, s, pt: (0, 0)),
            pl.BlockSpec(uo.shape, lambda b, s, pt: (0, 0)),
        ] + page_specs() + page_specs(),
        out_specs=pl.BlockSpec((1, 1, D_MODEL), lambda b, s, pt: (b, 0, 0)),
        scratch_shapes=[pltpu.VMEM((N_HEADS, SB_BLOCK), F32), pltpu.VMEM((N_HEADS, D_MODEL), F32)],
    )
    out = pl.pallas_call(
        functools.partial(_sb_decode_kernel, pages_per_step=pages_per_step),
        grid_spec=grid_spec,
        out_shape=jax.ShapeDtypeStruct((nb, 1, D_MODEL), F32),
        compiler_params=pltpu.CompilerParams(dimension_semantics=("parallel", "arbitrary")),
        name="sb_decode",
    )(pt, proj, bias_b, uo, *([ck] * pages_per_step), *([cv] * pages_per_step))
    return out.reshape(nb, D_MODEL)


def _merge_kernel(oa_ref, ob_ref, ga_ref, gb_ref, x_ref, wa_ref, wb_ref, wo_ref, gf_ref,
                  x1_ref, h2_ref):
    a = _dot(oa_ref[...].astype(BF16), wa_ref[...])
    b = _dot(ob_ref[...].astype(BF16), wb_ref[...])
    m = jax.nn.sigmoid(ga_ref[0]) * a + jax.nn.sigmoid(gb_ref[0]) * b
    x1 = x_ref[...] + _dot(m.astype(BF16), wo_ref[...])
    x1_ref[...] = x1
    h2_ref[...] = _rms(x1, gf_ref[...]).astype(BF16)


def _merge(o_a, o_b, proj, x, wa, wb, wo, g_ffn, tm):
    n = x.shape[0]
    row = pl.BlockSpec((tm, D_MODEL), lambda i: (i, 0))
    wspec = pl.BlockSpec((D_MODEL, D_MODEL), lambda i: (0, 0))
    return pl.pallas_call(
        _merge_kernel,
        grid=(n // tm,),
        in_specs=[
            row, row,
            pl.BlockSpec((1, tm, D_MODEL), lambda i: (G_GA, i, 0)),
            pl.BlockSpec((1, tm, D_MODEL), lambda i: (G_GB, i, 0)),
            row, wspec, wspec, wspec,
            pl.BlockSpec((1, D_MODEL), lambda i: (0, 0)),
        ],
        out_specs=[row, row],
        out_shape=[jax.ShapeDtypeStruct((n, D_MODEL), F32), jax.ShapeDtypeStruct((n, D_MODEL), BF16)],
        compiler_params=pltpu.CompilerParams(dimension_semantics=("parallel",)),
        name="merge",
    )(o_a, o_b, proj, proj, x, wa, wb, wo, g_ffn)


def _top_rows(vals, order, payload, n):
    out_v, out_p = [], []
    for _ in range(n):
        m = jnp.max(vals, axis=0, keepdims=True)
        first = jnp.min(jnp.where(vals == m, order, 1e9), axis=0, keepdims=True)
        hit = order == first
        out_v.append(m)
        if payload is None:
            out_p.append(first)
        else:
            out_p.append(jnp.max(jnp.where(hit, payload, -1.0), axis=0, keepdims=True))
        vals = jnp.where(hit, NEG_INF, vals)
    return out_v, out_p


def _stack_rows(rows, sub):
    out = jnp.zeros(sub.shape, F32)
    for r, row in enumerate(rows):
        out = jnp.where(sub == r, row, out)
    return out


def _peer_route_kernel(h_ref, wq_ref, sk_ref, i1_ref, i2_ref, gate_ref):
    t = h_ref.shape[0]
    q = _dot(h_ref[...], wq_ref[...]).astype(BF16)
    key_iota = lax.broadcasted_iota(jnp.int32, (PEER_KEYS, t), 0).astype(F32)
    sub16 = lax.broadcasted_iota(jnp.int32, (PEER_TOPK, t), 0)
    sub16f = sub16.astype(F32)
    sub8 = lax.broadcasted_iota(jnp.int32, (8, t), 0)
    sub8f = sub8.astype(F32)
    e_rows, g_rows = [], []
    for h in range(N_HEADS):
        top_v, top_i = [], []
        for p in range(2):
            c0 = (h * 2 + p) * PEER_KEYS
            s = _dot_nt(sk_ref[p], q[:, c0:c0 + PEER_KEYS])
            v, i = _top_rows(s, key_iota, None, PEER_TOPK)
            top_v.append(v)
            top_i.append(i)
        v2 = (_stack_rows(top_v[1][:8], sub8), _stack_rows(top_v[1], sub16))
        i2 = (_stack_rows(top_i[1][:8], sub8), _stack_rows(top_i[1], sub16))
        cv, co, ce = [], [], []
        for a in range(PEER_TOPK):
            nb = PEER_TOPK // (a + 1)
            wide = int(a == 0)
            sub, subf = (sub16, sub16f) if wide else (sub8, sub8f)
            cv.append(jnp.where(sub < nb, top_v[0][a] + v2[wide], NEG_INF))
            co.append(subf + float(a * PEER_TOPK))
            ce.append(top_i[0][a] * float(PEER_KEYS) + i2[wide])
        best, experts = _top_rows(jnp.concatenate(cv, axis=0), jnp.concatenate(co, axis=0),
                                  jnp.concatenate(ce, axis=0), PEER_TOPK)
        top = best[0]
        best = _stack_rows(best, sub16)
        ex = jnp.exp(best - top)
        g_rows.append(ex / jnp.sum(ex, axis=0, keepdims=True))
        e_rows.append(_stack_rows(experts, sub16))
    experts = jnp.concatenate(e_rows, axis=0)
    gates = jnp.concatenate(g_rows, axis=0)
    key1 = jnp.floor(experts * (1.0 / PEER_KEYS))
    i1_ref[...] = key1.T
    i2_ref[...] = (experts - key1 * float(PEER_KEYS)).T
    gate_ref[...] = gates.T


def _peer_route(h2, wq, sub_keys, tt):
    n = h2.shape[0]
    sel = pl.BlockSpec((tt, PEER_SEL), lambda i: (i, 0))
    shp = jax.ShapeDtypeStruct((n, PEER_SEL), F32)
    return pl.pallas_call(
        _peer_route_kernel,
        grid=(n // tt,),
        in_specs=[
            pl.BlockSpec((tt, D_MODEL), lambda i: (i, 0)),
            pl.BlockSpec(wq.shape, lambda i: (0, 0)),
            pl.BlockSpec(sub_keys.shape, lambda i: (0, 0, 0)),
        ],
        out_specs=[sel, sel, sel],
        out_shape=[shp, shp, shp],
        compiler_params=pltpu.CompilerParams(dimension_semantics=("parallel",)),
        name="peer_route",
    )(h2, wq, sub_keys)


def _slab_pitch(tt):
    p = tt // 8
    return 8 * (p + 1 if p % 2 == 0 else p + 2)


def _peer_expert_kernel(h_ref, ut_ref, v_ref, i1_ref, i2_ref, gate_ref, x_ref, gf_ref,
                        y_ref, slab_ref, sel_ref, acc_ref, *, n_chunks, pitch, final_norm):
    s = pl.program_id(1)
    tt = h_ref.shape[0]
    slabs_per_chunk = ut_ref.shape[1] // PEER_KEYS
    key_iota = lax.broadcasted_iota(jnp.int32, (PEER_KEYS, PEER_SEL), 0).astype(F32)

    def slab_rows(g):
        return pl.ds(pl.multiple_of(g * pitch, 8), tt)

    def token_rows(t):
        return pl.ds(t, PEER_KEYS, stride=pitch)

    @pl.when(s < n_chunks)
    def _():
        a = _dot(h_ref[...], ut_ref[...])
        for gl in range(slabs_per_chunk):
            slab_ref[slab_rows(s * slabs_per_chunk + gl), :] = a[:, gl * PEER_KEYS:(gl + 1) * PEER_KEYS]

    @pl.when(s == n_chunks - 1)
    def _():
        def gather(t, c):
            pre = slab_ref[token_rows(t), :]
            hit2 = (key_iota == i2_ref[pl.ds(t, 1), :]).astype(BF16)
            hi, lo = _split2(pre)
            picked = _dot(hi, hit2) + _dot(lo, hit2)
            hit1 = key_iota == i1_ref[pl.ds(t, 1), :]
            sel_ref[pl.ds(t, 1), :] = jnp.sum(jnp.where(hit1, picked, 0.0), axis=0, keepdims=True)
            return c

        lax.fori_loop(0, tt, gather, 0)
        pre = sel_ref[...]
        sel_ref[...] = gate_ref[...] * (0.5 * pre * (1.0 + lax.erf(pre * (0.5 ** 0.5))))

        def scatter(t, c):
            hit2 = (key_iota == i2_ref[pl.ds(t, 1), :]).astype(BF16)
            w1 = jnp.where(key_iota == i1_ref[pl.ds(t, 1), :], sel_ref[pl.ds(t, 1), :], 0.0).astype(BF16)
            slab_ref[token_rows(t), :] = _dot_nt(w1, hit2)
            return c

        lax.fori_loop(0, tt, scatter, 0)
        acc_ref[...] = jnp.zeros_like(acc_ref)

    @pl.when(s >= n_chunks)
    def _():
        c = s - n_chunks
        w = jnp.concatenate(
            [slab_ref[slab_rows(c * slabs_per_chunk + gl), :].astype(BF16) for gl in range(slabs_per_chunk)],
            axis=1)
        acc_ref[...] += _dot(w, v_ref[...])

    @pl.when(s == 2 * n_chunks - 1)
    def _():
        x2 = x_ref[...] + acc_ref[...]
        y_ref[...] = _rms(x2, gf_ref[...]) if final_norm else x2


def _peer_experts(h2, u_t, v_tab, i1, i2, gates, x1, g_final, final_norm, tt, chunk):
    n = h2.shape[0]
    n_exp = v_tab.shape[0]
    n_chunks = n_exp // chunk
    pitch = _slab_pitch(tt)
    row = pl.BlockSpec((tt, D_MODEL), lambda i, s: (i, 0))
    sel = pl.BlockSpec((tt, PEER_SEL), lambda i, s: (i, 0))
    return pl.pallas_call(
        functools.partial(_peer_expert_kernel, n_chunks=n_chunks, pitch=pitch, final_norm=final_norm),
        grid=(n // tt, 2 * n_chunks),
        in_specs=[
            row,
            pl.BlockSpec((D_MODEL, chunk), lambda i, s: (0, jnp.minimum(s, n_chunks - 1))),
            pl.BlockSpec((chunk, D_MODEL), lambda i, s: (jnp.maximum(s - n_chunks, 0), 0)),
            sel, sel, sel, row,
            pl.BlockSpec((1, D_MODEL), lambda i, s: (0, 0)),
        ],
        out_specs=row,
        out_shape=jax.ShapeDtypeStruct((n, D_MODEL), F32),
        scratch_shapes=[
            pltpu.VMEM((PEER_KEYS * pitch, PEER_KEYS), F32),
            pltpu.VMEM((tt, PEER_SEL), F32),
            pltpu.VMEM((tt, D_MODEL), F32),
        ],
        compiler_params=pltpu.CompilerParams(dimension_semantics=("parallel", "arbitrary")),
        name="peer_experts",
    )(h2, u_t, v_tab, i1, i2, gates, x1, g_final)


def _tile(n, target):
    return min(n, target)


def kernel(x_prompt, x_sample, cache_k, cache_v, state_hgrn, page_table, g_attn, w_in, hg_lb_logits,
           hg_norm, sb_bias, w_branch_a, w_branch_b, w_out, g_ffn, peer_wq, peer_subkeys, peer_u, peer_v,
           g_final):
    depth = w_in.shape[0]
    batch, seq, _ = x_prompt.shape
    nb, dec_seq, _ = x_sample.shape
    assert dec_seq == 1
    n_p = batch * seq
    xp = x_prompt.reshape(n_p, D_MODEL)
    xs = x_sample.reshape(nb, D_MODEL)
    g_fin = g_final.reshape(1, D_MODEL)
    outs = {k: [] for k in ("sp", "kp", "vp", "ss", "ks", "vs")}
    for l in range(depth):
        w_in_l = w_in[l].astype(BF16)
        wa, wb, wo = (w[l].astype(BF16) for w in (w_branch_a, w_branch_b, w_out))
        wq = peer_wq[l].astype(BF16)
        sk = peer_subkeys[l].astype(BF16)
        u_t = peer_u[l].astype(BF16).T
        v_tab = peer_v[l].astype(BF16)
        g_a = g_attn[l].reshape(1, D_MODEL)
        g_f = g_ffn[l].reshape(1, D_MODEL)
        ng = hg_norm[l].reshape(1, D_MODEL)
        bias_b = jnp.broadcast_to(sb_bias[l].astype(F32)[:, None], (N_HEADS, SB_BLOCK))
        last = l == depth - 1

        proj = _norm_proj(xp, g_a, w_in_l, _tile(n_p, 1024))
        o_a, s_p = _hgrn_prompt(proj, hg_lb_logits, ng, batch, seq, l, _tile(seq, 512))
        o_b = _sb_prompt(proj, bias_b, batch, seq)
        x1, h2 = _merge(o_a, o_b, proj, xp, wa, wb, wo, g_f, _tile(n_p, 512))
        i1, i2, gates = _peer_route(h2, wq, sk, _tile(n_p, 256))
        xp = _peer_experts(h2, u_t, v_tab, i1, i2, gates, x1, g_fin, last, _tile(n_p, 256), 2048)
        outs["sp"].append(s_p)
        outs["kp"].append(proj[G_SK].reshape(batch, seq, N_HEADS, HEAD_DIM))
        outs["vp"].append(proj[G_SV].reshape(batch, seq, N_HEADS, HEAD_DIM))

        proj_s = _norm_proj(xs, g_a, w_in_l, nb)
        o_a_s, s_s = _hgrn_step(proj_s, hg_lb_logits, hg_norm[l], state_hgrn[l], l)
        o_b_s = _sb_decode(proj_s, bias_b, cache_k[l], cache_v[l], page_table, 8)
        x1_s, h2_s = _merge(o_a_s.reshape(nb, D_MODEL), o_b_s, proj_s, xs, wa, wb, wo, g_f, nb)
        pad = (-nb) % PEER_KEYS
        sel_s = _peer_route(jnp.pad(h2_s, ((0, pad), (0, 0))), wq, sk, PEER_KEYS)
        i1_s, i2_s, gates_s = (a[:nb] for a in sel_s)
        xs = _peer_experts(h2_s, u_t, v_tab, i1_s, i2_s, gates_s, x1_s, g_fin, last, nb, 2048)
        outs["ss"].append(s_s)
        outs["ks"].append(proj_s[G_SK].reshape(nb, 1, N_HEADS, HEAD_DIM))
        outs["vs"].append(proj_s[G_SV].reshape(nb, 1, N_HEADS, HEAD_DIM))

    y_prompt = xp.reshape(batch, seq, D_MODEL)
    y_sample = xs.reshape(nb, 1, D_MODEL)
    return (y_prompt, y_sample, jnp.stack(outs["sp"]), jnp.stack(outs["kp"]), jnp.stack(outs["vp"]),
            jnp.stack(outs["ss"]), jnp.stack(outs["ks"]), jnp.stack(outs["vs"]))
```

```python
import functools

import numpy as np
import jax
import jax.numpy as jnp
from jax import lax
from jax.experimental import pallas as pl
from jax.experimental.pallas import tpu as pltpu

F32 = jnp.float32
BF16 = jnp.bfloat16

D_MODEL = 1024
N_HEADS = 8
HEAD_DIM = 128
N_GROUPS = 9
HG_CHUNK = 64
SB_BLOCK = 128
SB_TILE = 512
SB_SPAN = 256
PEER_GROUP = 8
PEER_GROUP_UNROLL = 2
PEER_KEYS = 128
PEER_TOPK = 16
PEER_SEL = N_HEADS * PEER_TOPK
RMS_EPS = 1e-6
NEG_INF = float("-inf")

G_HQ, G_HF, G_HI, G_HG, G_SQ, G_SK, G_SV, G_GA, G_GB = range(N_GROUPS)


def _dot(a, b):
    return jnp.dot(a, b, preferred_element_type=F32)


def _dot_nt(a, b):
    return lax.dot_general(a, b, (((1,), (1,)), ((), ())), preferred_element_type=F32)


def _dot_tn(a, b):
    return lax.dot_general(a, b, (((0,), (0,)), ((), ())), preferred_element_type=F32)


def _split2(x):
    hi = x.astype(BF16)
    lo = (x - hi.astype(F32)).astype(BF16)
    return hi, lo


def _split3(x):
    hi = x.astype(BF16)
    r = x - hi.astype(F32)
    mid = r.astype(BF16)
    lo = (r - mid.astype(F32)).astype(BF16)
    return hi, mid, lo


def _exact_dot_lhs01(m01, x):
    n = x.shape[1]
    parts = _dot(m01, jnp.concatenate(_split3(x), axis=1))
    return parts[:, :n] + parts[:, n:2 * n] + parts[:, 2 * n:]


def _exact_dot_rhs01(x, m01):
    hi, mid, lo = _split3(x)
    return _dot(hi, m01) + _dot(mid, m01) + _dot(lo, m01)


def _rms(x, g):
    return x * lax.rsqrt(jnp.mean(x * x, axis=-1, keepdims=True) + RMS_EPS) * g


def _silu(x):
    return x * jax.nn.sigmoid(x)


def _norm_proj_kernel(x_ref, g_ref, w_ref, o_ref, k_ref, v_ref, h_ref):
    j = pl.program_id(1)
    tm = x_ref.shape[0]

    @pl.when(j == 0)
    def _():
        h_ref[...] = _rms(x_ref[...], g_ref[...]).astype(BF16)

    out = _dot(h_ref[...], w_ref[...])
    o_ref[0] = out

    for group, ref in ((G_SK, k_ref), (G_SV, v_ref)):
        @pl.when(j == group)
        def _():
            for h in range(N_HEADS):
                ref[pl.ds(h, tm, stride=N_HEADS), :] = out[:, h * HEAD_DIM:(h + 1) * HEAD_DIM]


def _norm_proj(x, g, w, tm):
    n = x.shape[0]
    kv_spec = pl.BlockSpec((tm * N_HEADS, HEAD_DIM), lambda i, j: (i, 0))
    kv_shape = jax.ShapeDtypeStruct((n * N_HEADS, HEAD_DIM), F32)
    return pl.pallas_call(
        _norm_proj_kernel,
        grid=(n // tm, N_GROUPS),
        in_specs=[
            pl.BlockSpec((tm, D_MODEL), lambda i, j: (i, 0)),
            pl.BlockSpec((1, D_MODEL), lambda i, j: (0, 0)),
            pl.BlockSpec((D_MODEL, D_MODEL), lambda i, j: (0, j)),
        ],
        out_specs=[pl.BlockSpec((1, tm, D_MODEL), lambda i, j: (j, i, 0)), kv_spec, kv_spec],
        out_shape=[jax.ShapeDtypeStruct((N_GROUPS, n, D_MODEL), F32), kv_shape, kv_shape],
        scratch_shapes=[pltpu.VMEM((tm, D_MODEL), BF16)],
        compiler_params=pltpu.CompilerParams(dimension_semantics=("parallel", "arbitrary")),
        name="norm_proj",
    )(x, g, w)


def _hgrn_consts():
    c = HG_CHUNK
    t = np.arange(c)[:, None]
    j = np.arange(c)[None, :]
    sel = [(j <= t)]
    masks = []
    m = c // 2
    while m >= 1:
        mid = (t // (2 * m)) * (2 * m) + m
        upper = t >= mid
        d = np.where(upper, (j >= mid) & (j <= t), (j > t) & (j <= mid - 1))
        sel.append(d)
        s = j
        same = (t // (2 * m)) == (s // (2 * m))
        s_mid = (s // (2 * m)) * (2 * m) + m
        masks.append(same & upper & (s < s_mid))
        m //= 2
    sel.append(j > t)
    masks.append(t == j)
    sel = np.concatenate(sel, axis=0).astype(np.float32)
    masks = np.stack(masks).astype(np.float32)
    return jnp.asarray(sel, dtype=BF16), jnp.asarray(masks, dtype=F32)


def _lower_bound(lbl, layer):
    e = jnp.exp(lbl - jnp.max(lbl, axis=0, keepdims=True))
    return jnp.sum(e[: layer + 1], axis=0, keepdims=True) / jnp.sum(e, axis=0, keepdims=True)


def _hgrn_kernel(q_ref, f_ref, i_ref, g_ref, lbl_ref, ng_ref, sel_ref, mask_ref,
                 o_ref, s_ref, st_ref, *, n_sub, layer):
    c = pl.program_id(2)
    n_lvl = mask_ref.shape[0] - 1
    ch = HG_CHUNK

    @pl.when(c == 0)
    def _():
        st_ref[...] = jnp.zeros_like(st_ref)

    lb = _lower_bound(lbl_ref[...], layer)
    ng = ng_ref[...]
    sel = sel_ref[...]

    chunks = []
    for sc in range(n_sub):
        rows = slice(sc * ch, (sc + 1) * ch)
        f = lb + (1.0 - lb) * jax.nn.sigmoid(f_ref[0, rows, :])
        logf = jnp.log(f)
        k = 1.0 - f
        q = _silu(q_ref[0, rows, :])
        v = i_ref[0, rows, :].astype(BF16)
        ex = jnp.exp(_exact_dot_lhs01(sel, logf))
        scores = _dot_nt(q.astype(BF16), k.astype(BF16)) * mask_ref[n_lvl]
        for li in range(n_lvl):
            el = ex[(li + 1) * ch:(li + 2) * ch]
            scores += _dot_nt((q * el).astype(BF16), (k * el).astype(BF16)) * mask_ref[li]
        e_last = ex[(n_lvl + 1) * ch:(n_lvl + 2) * ch]
        chunks.append((
            (q * ex[0:ch]).astype(BF16),
            _dot(scores.astype(BF16), v),
            ex[ch - 1:ch],
            _dot_tn(v, (k * e_last).astype(BF16)),
        ))

    st = st_ref[...]
    for sc, (q_dec, o_intra, decay, kv) in enumerate(chunks):
        rows = slice(sc * ch, (sc + 1) * ch)
        o = o_intra + _dot_nt(q_dec, st.astype(BF16))
        st = st * decay + kv
        o = o * lax.rsqrt(jnp.mean(o * o, axis=-1, keepdims=True) + RMS_EPS)
        o_ref[rows, :] = (o * ng * _silu(g_ref[0, rows, :])).astype(o_ref.dtype)
    st_ref[...] = st

    @pl.when(c == pl.num_programs(2) - 1)
    def _():
        s_ref[0, 0] = st_ref[...].T


def _hgrn_prompt(proj, lb_logits, norm_g, batch, seq, layer, tc):
    sel, masks = _hgrn_consts()
    nc = seq // tc
    n = batch * seq

    def pmap(g):
        return lambda b, h, c: (g, b * nc + c, h)

    kern = functools.partial(_hgrn_kernel, n_sub=tc // HG_CHUNK, layer=layer)
    return pl.pallas_call(
        kern,
        grid=(batch, N_HEADS, nc),
        in_specs=[
            pl.BlockSpec((1, tc, HEAD_DIM), pmap(G_HQ)),
            pl.BlockSpec((1, tc, HEAD_DIM), pmap(G_HF)),
            pl.BlockSpec((1, tc, HEAD_DIM), pmap(G_HI)),
            pl.BlockSpec((1, tc, HEAD_DIM), pmap(G_HG)),
            pl.BlockSpec((lb_logits.shape[0], HEAD_DIM), lambda b, h, c: (0, h)),
            pl.BlockSpec((1, HEAD_DIM), lambda b, h, c: (0, h)),
            pl.BlockSpec(sel.shape, lambda b, h, c: (0, 0)),
            pl.BlockSpec(masks.shape, lambda b, h, c: (0, 0, 0)),
        ],
        out_specs=[
            pl.BlockSpec((tc, HEAD_DIM), lambda b, h, c: (b * nc + c, h)),
            pl.BlockSpec((1, 1, HEAD_DIM, HEAD_DIM), lambda b, h, c: (b, h, 0, 0)),
        ],
        out_shape=[
            jax.ShapeDtypeStruct((n, D_MODEL), BF16),
            jax.ShapeDtypeStruct((batch, N_HEADS, HEAD_DIM, HEAD_DIM), F32),
        ],
        scratch_shapes=[pltpu.VMEM((HEAD_DIM, HEAD_DIM), F32)],
        compiler_params=pltpu.CompilerParams(
            dimension_semantics=("parallel", "parallel", "arbitrary")),
        name="hgrn_prompt",
    )(proj, proj, proj, proj, lb_logits, norm_g, sel, masks)


def _hgrn_step_kernel(q_ref, f_ref, i_ref, g_ref, lbl_ref, ng_ref, s_ref, o_ref, sn_ref, *, layer):
    lb = _lower_bound(lbl_ref[...], layer)
    lb = lb[0]
    f = lb + (1.0 - lb) * jax.nn.sigmoid(f_ref[0, 0])
    q = _silu(q_ref[0, 0])
    v = i_ref[0, 0]
    g = g_ref[0, 0]
    ng = ng_ref[...]
    pad = jnp.zeros((HEAD_DIM - 2 * N_HEADS, HEAD_DIM), F32)
    cols = jnp.concatenate([f, q, pad], axis=0).T
    for h in range(N_HEADS):
        fcol = cols[:, h:h + 1]
        qcol = cols[:, N_HEADS + h:N_HEADS + h + 1]
        s_new = fcol * s_ref[0, h] + (1.0 - fcol) * v[h:h + 1, :]
        sn_ref[0, h] = s_new
        o = jnp.sum(qcol * s_new, axis=0, keepdims=True)
        o = o * lax.rsqrt(jnp.mean(o * o, axis=-1, keepdims=True) + RMS_EPS)
        o_ref[0, pl.ds(h, 1), :] = o * ng[h:h + 1, :] * _silu(g[h:h + 1, :])


def _hgrn_step(proj, lb_logits, norm_g, state, layer):
    nb = state.shape[0]
    proj4 = proj.reshape(N_GROUPS, nb, N_HEADS, HEAD_DIM)
    lbl = lb_logits.reshape(lb_logits.shape[0], N_HEADS, HEAD_DIM)
    ng = norm_g.reshape(N_HEADS, HEAD_DIM)

    def pmap(g):
        return lambda b: (g, b, 0, 0)

    blk = (1, 1, N_HEADS, HEAD_DIM)
    return pl.pallas_call(
        functools.partial(_hgrn_step_kernel, layer=layer),
        grid=(nb,),
        in_specs=[
            pl.BlockSpec(blk, pmap(G_HQ)),
            pl.BlockSpec(blk, pmap(G_HF)),
            pl.BlockSpec(blk, pmap(G_HI)),
            pl.BlockSpec(blk, pmap(G_HG)),
            pl.BlockSpec(lbl.shape, lambda b: (0, 0, 0)),
            pl.BlockSpec(ng.shape, lambda b: (0, 0)),
            pl.BlockSpec((1, N_HEADS, HEAD_DIM, HEAD_DIM), lambda b: (b, 0, 0, 0)),
        ],
        out_specs=[
            pl.BlockSpec((1, N_HEADS, HEAD_DIM), lambda b: (b, 0, 0)),
            pl.BlockSpec((1, N_HEADS, HEAD_DIM, HEAD_DIM), lambda b: (b, 0, 0, 0)),
        ],
        out_shape=[
            jax.ShapeDtypeStruct((nb, N_HEADS, HEAD_DIM), F32),
            jax.ShapeDtypeStruct(state.shape, F32),
        ],
        name="hgrn_step",
    )(proj4, proj4, proj4, proj4, lbl, ng, state)


def _later_sum_consts():
    j = np.arange(SB_BLOCK)[:, None]
    k = np.arange(SB_BLOCK)[None, :]
    u = np.concatenate([(j > k), np.ones((SB_BLOCK, SB_BLOCK), bool)], axis=1)
    return jnp.asarray(u.astype(np.float32), dtype=BF16)


def _log_beta_terms(z):
    l1p = jnp.log(1.0 + jnp.exp(-jnp.abs(z)))
    return jnp.minimum(z, 0.0) - l1p, -jnp.maximum(z, 0.0) - l1p


def _strictly_later_const(span):
    j = np.arange(span)[:, None]
    k = np.arange(span)[None, :]
    return jnp.asarray((j > k).astype(np.float32), dtype=BF16)


def _mask_last_block(x, visible):
    cut = x.shape[1] - SB_BLOCK
    last = jnp.where(visible, x[:, cut:], 0.0)
    return last if cut == 0 else jnp.concatenate([x[:, :cut], last], axis=1)


def _sb_rows(q, k, v, carry, acc, later01, bias, visible):
    blk = SB_BLOCK
    span = later01.shape[0]
    n_keys = k.shape[0]
    z = _dot_nt(q, k) * (HEAD_DIM ** -0.5) + bias
    log_b, log_keep = _log_beta_terms(z)
    if visible is not None:
        log_keep = _mask_last_block(log_keep, visible)
    ws = []
    for c0 in reversed(range(0, n_keys, span)):
        wd = min(span, n_keys - c0)
        lk = log_keep[:, c0:c0 + wd]
        u = later01 if wd == span else later01[:wd, :wd]
        hi, lo = _split2(lk)
        later = _dot(hi, u) + _dot(lo, u)
        ws.insert(0, jnp.exp(log_b[:, c0:c0 + wd] + later + jnp.concatenate([carry] * (wd // blk), axis=1)))
        carry = carry + (later[:, 0:1] + lk[:, 0:1])
    w = jnp.concatenate(ws, axis=1)
    if visible is not None:
        w = _mask_last_block(w, visible)
    return carry, acc + _dot(w.astype(BF16), v)


def _sb_prompt_kernel(bias_ref, q_ref, k_ref, v_ref, uo_ref, o_ref, *, tile):
    h = pl.program_id(1)
    qi = pl.program_id(2)
    blk = SB_BLOCK
    nsub = tile // blk
    q = q_ref[0].astype(BF16)
    bias = bias_ref[h]
    uo = uo_ref[...]
    visible = (lax.broadcasted_iota(jnp.int32, (blk, blk), 1)
               < lax.broadcasted_iota(jnp.int32, (blk, blk), 0))

    def keys(kt):
        rows = pl.ds(pl.multiple_of(kt * tile, tile), tile)
        return k_ref[0, rows, :].astype(BF16), v_ref[0, rows, :].astype(BF16)

    k, v = keys(qi)
    zeros = jnp.zeros((blk, blk), F32)
    parts = [_sb_rows(q[r * blk:(r + 1) * blk], k[:(r + 1) * blk], v[:(r + 1) * blk],
                      zeros, zeros, uo, bias, visible) for r in range(nsub)]
    carry = jnp.concatenate([p[0] for p in parts], axis=0)
    acc = jnp.concatenate([p[1] for p in parts], axis=0)

    def body(s, ca):
        k, v = keys(qi - 1 - s)
        return _sb_rows(q, k, v, ca[0], ca[1], uo, bias, None)

    carry, acc = lax.fori_loop(0, qi, body, (carry, acc))
    o_ref[...] = acc.astype(o_ref.dtype)


def _sb_prompt(proj, bias, batch, seq, tile):
    nq = seq // tile
    n = batch * seq
    uo = _strictly_later_const(min(tile, SB_SPAN))
    return pl.pallas_call(
        functools.partial(_sb_prompt_kernel, tile=tile),
        grid=(batch, N_HEADS, nq),
        in_specs=[
            pl.BlockSpec(memory_space=pltpu.SMEM),
            pl.BlockSpec((1, tile, HEAD_DIM), lambda b, h, i: (G_SQ, b * nq + i, h)),
            pl.BlockSpec((1, seq, HEAD_DIM), lambda b, h, i: (G_SK, b, h)),
            pl.BlockSpec((1, seq, HEAD_DIM), lambda b, h, i: (G_SV, b, h)),
            pl.BlockSpec(uo.shape, lambda b, h, i: (0, 0)),
        ],
        out_specs=pl.BlockSpec((tile, HEAD_DIM), lambda b, h, i: (b * nq + i, h)),
        out_shape=jax.ShapeDtypeStruct((n, D_MODEL), BF16),
        compiler_params=pltpu.CompilerParams(
            dimension_semantics=("parallel", "parallel", "arbitrary")),
        name="sb_prompt",
    )(bias, proj, proj, proj, uo)


def _sb_decode_kernel(pt_ref, q_ref, bias_ref, uo_ref, *refs, pages_per_step):
    k_refs = refs[:pages_per_step]
    v_refs = refs[pages_per_step:2 * pages_per_step]
    o_ref, carry_ref, acc_ref = refs[2 * pages_per_step:]
    b = pl.program_id(0)
    s = pl.program_id(1)
    scale = HEAD_DIM ** -0.5
    own_head = (jnp.right_shift(lax.broadcasted_iota(jnp.int32, (N_HEADS, D_MODEL), 1), 7)
                == lax.broadcasted_iota(jnp.int32, (N_HEADS, D_MODEL), 0))

    @pl.when(s == 0)
    def _():
        carry_ref[...] = jnp.zeros_like(carry_ref)
        acc_ref[...] = jnp.zeros_like(acc_ref)

    qrow = q_ref[0, pl.ds(b, 1), :]
    q_bd = jnp.where(own_head, jnp.broadcast_to(qrow, (N_HEADS, D_MODEL)), 0.0).astype(BF16)
    bias = bias_ref[...]
    uo = uo_ref[...]
    carry = carry_ref[...]
    blk = SB_BLOCK
    k_all = jnp.concatenate([k_refs[r][0].astype(BF16) for r in range(pages_per_step)], axis=0)
    v_all = jnp.concatenate([v_refs[r][0].astype(BF16) for r in range(pages_per_step)], axis=0)
    z = _dot_nt(q_bd, k_all) * scale + jnp.concatenate([bias] * pages_per_step, axis=1)
    log_b, log_keep = _log_beta_terms(z)
    lk_rows = jnp.concatenate([log_keep[:, r * blk:(r + 1) * blk] for r in range(pages_per_step)], axis=0)
    cs = _exact_dot_rhs01(lk_rows, uo)
    ws = []
    for r in range(pages_per_step):
        rows = slice(r * N_HEADS, (r + 1) * N_HEADS)
        ws.append(jnp.exp(log_b[:, r * blk:(r + 1) * blk] + cs[rows, :blk] + carry).astype(BF16))
        carry = carry + cs[rows, blk:]
    carry_ref[...] = carry
    acc = acc_ref[...] + _dot(jnp.concatenate(ws, axis=1), v_all)
    acc_ref[...] = acc

    @pl.when(s == pl.num_programs(1) - 1)
    def _():
        o_ref[0] = jnp.sum(jnp.where(own_head, acc, 0.0), axis=0, keepdims=True)


def _sb_decode(proj, bias_b, cache_k, cache_v, page_table, pages_per_step):
    nb, n_pages = page_table.shape
    n_pool, page = cache_k.shape[0], cache_k.shape[1]
    ck = cache_k.reshape(n_pool, page, D_MODEL)
    cv = cache_v.reshape(n_pool, page, D_MODEL)
    uo = _later_sum_consts()
    pt = page_table.reshape(-1)

    def page_map(r):
        return lambda b, s, pt: (pt[b * n_pages + n_pages - 1 - (s * pages_per_step + r)], 0, 0)

    def page_specs():
        return [pl.BlockSpec((1, page, D_MODEL), page_map(r)) for r in range(pages_per_step)]

    grid_spec = pltpu.PrefetchScalarGridSpec(
        num_scalar_prefetch=1,
        grid=(nb, n_pages // pages_per_step),
        in_specs=[
            pl.BlockSpec((1, nb, D_MODEL), lambda b, s, pt: (G_SQ, 0, 0)),
            pl.BlockSpec(bias_b.shape, lambda b, s, pt: (0, 0)),
            pl.BlockSpec(uo.shape, lambda b, s, pt: (0, 0)),
        ] + page_specs() + page_specs(),
        out_specs=pl.BlockSpec((1, 1, D_MODEL), lambda b, s, pt: (b, 0, 0)),
        scratch_shapes=[pltpu.VMEM((N_HEADS, SB_BLOCK), F32), pltpu.VMEM((N_HEADS, D_MODEL), F32)],
    )
    out = pl.pallas_call(
        functools.partial(_sb_decode_kernel, pages_per_step=pages_per_step),
        grid_spec=grid_spec,
        out_shape=jax.ShapeDtypeStruct((nb, 1, D_MODEL), F32),
        compiler_params=pltpu.CompilerParams(dimension_semantics=("parallel", "arbitrary")),
        name="sb_decode",
    )(pt, proj, bias_b, uo, *([ck] * pages_per_step), *([cv] * pages_per_step))
    return out.reshape(nb, D_MODEL)


def _merge_kernel(oa_ref, ob_ref, ga_ref, gb_ref, x_ref, wa_ref, wb_ref, wo_ref, gf_ref,
                  x1_ref, h2_ref):
    a = _dot(oa_ref[...].astype(BF16), wa_ref[...])
    b = _dot(ob_ref[...].astype(BF16), wb_ref[...])
    m = jax.nn.sigmoid(ga_ref[0]) * a + jax.nn.sigmoid(gb_ref[0]) * b
    x1 = x_ref[...] + _dot(m.astype(BF16), wo_ref[...])
    x1_ref[...] = x1
    h2_ref[...] = _rms(x1, gf_ref[...]).astype(BF16)


def _merge(o_a, o_b, proj, x, wa, wb, wo, g_ffn, tm):
    n = x.shape[0]
    row = pl.BlockSpec((tm, D_MODEL), lambda i: (i, 0))
    wspec = pl.BlockSpec((D_MODEL, D_MODEL), lambda i: (0, 0))
    return pl.pallas_call(
        _merge_kernel,
        grid=(n // tm,),
        in_specs=[
            row, row,
            pl.BlockSpec((1, tm, D_MODEL), lambda i: (G_GA, i, 0)),
            pl.BlockSpec((1, tm, D_MODEL), lambda i: (G_GB, i, 0)),
            row, wspec, wspec, wspec,
            pl.BlockSpec((1, D_MODEL), lambda i: (0, 0)),
        ],
        out_specs=[row, row],
        out_shape=[jax.ShapeDtypeStruct((n, D_MODEL), F32), jax.ShapeDtypeStruct((n, D_MODEL), BF16)],
        compiler_params=pltpu.CompilerParams(dimension_semantics=("parallel",)),
        name="merge",
    )(o_a, o_b, proj, proj, x, wa, wb, wo, g_ffn)


def _top_rows(vals, order, payload, n):
    out_v, out_p = [], []
    for _ in range(n):
        m = jnp.max(vals, axis=0, keepdims=True)
        first = jnp.min(jnp.where(vals == m, order, 1e9), axis=0, keepdims=True)
        hit = order == first
        out_v.append(m)
        if payload is None:
            out_p.append(first)
        else:
            out_p.append(jnp.max(jnp.where(hit, payload, -1.0), axis=0, keepdims=True))
        vals = jnp.where(hit, NEG_INF, vals)
    return out_v, out_p


def _stack_rows(rows, sub):
    out = jnp.zeros(sub.shape, F32)
    for r, row in enumerate(rows):
        out = jnp.where(sub == r, row, out)
    return out


def _peer_route_kernel(h_ref, wq_ref, sk_ref, i1_ref, i2_ref, gate_ref):
    t = h_ref.shape[0]
    q = _dot(h_ref[...], wq_ref[...]).astype(BF16)
    key_iota = lax.broadcasted_iota(jnp.int32, (PEER_KEYS, t), 0).astype(F32)
    sub16 = lax.broadcasted_iota(jnp.int32, (PEER_TOPK, t), 0)
    sub16f = sub16.astype(F32)
    sub8 = lax.broadcasted_iota(jnp.int32, (8, t), 0)
    sub8f = sub8.astype(F32)
    e_rows, g_rows = [], []
    for h in range(N_HEADS):
        top_v, top_i = [], []
        for p in range(2):
            c0 = (h * 2 + p) * PEER_KEYS
            s = _dot_nt(sk_ref[p], q[:, c0:c0 + PEER_KEYS])
            v, i = _top_rows(s, key_iota, None, PEER_TOPK)
            top_v.append(v)
            top_i.append(i)
        v2 = (_stack_rows(top_v[1][:8], sub8), _stack_rows(top_v[1], sub16))
        i2 = (_stack_rows(top_i[1][:8], sub8), _stack_rows(top_i[1], sub16))
        cv, co, ce = [], [], []
        for a in range(PEER_TOPK):
            nb = PEER_TOPK // (a + 1)
            wide = int(a == 0)
            sub, subf = (sub16, sub16f) if wide else (sub8, sub8f)
            cv.append(jnp.where(sub < nb, top_v[0][a] + v2[wide], NEG_INF))
            co.append(subf + float(a * PEER_TOPK))
            ce.append(top_i[0][a] * float(PEER_KEYS) + i2[wide])
        best, experts = _top_rows(jnp.concatenate(cv, axis=0), jnp.concatenate(co, axis=0),
                                  jnp.concatenate(ce, axis=0), PEER_TOPK)
        top = best[0]
        best = _stack_rows(best, sub16)
        ex = jnp.exp(best - top)
        g_rows.append(ex / jnp.sum(ex, axis=0, keepdims=True))
        e_rows.append(_stack_rows(experts, sub16))
    experts = jnp.concatenate(e_rows, axis=0)
    gates = jnp.concatenate(g_rows, axis=0)
    key1 = jnp.floor(experts * (1.0 / PEER_KEYS))
    i1_ref[...] = key1.T
    i2_ref[...] = (experts - key1 * float(PEER_KEYS)).T
    gate_ref[...] = gates.T


def _peer_route(h2, wq, sub_keys, tt):
    n = h2.shape[0]
    sel = pl.BlockSpec((tt, PEER_SEL), lambda i: (i, 0))
    shp = jax.ShapeDtypeStruct((n, PEER_SEL), F32)
    return pl.pallas_call(
        _peer_route_kernel,
        grid=(n // tt,),
        in_specs=[
            pl.BlockSpec((tt, D_MODEL), lambda i: (i, 0)),
            pl.BlockSpec(wq.shape, lambda i: (0, 0)),
            pl.BlockSpec(sub_keys.shape, lambda i: (0, 0, 0)),
        ],
        out_specs=[sel, sel, sel],
        out_shape=[shp, shp, shp],
        compiler_params=pltpu.CompilerParams(dimension_semantics=("parallel",)),
        name="peer_route",
    )(h2, wq, sub_keys)


def _slab_pitch(tt):
    p = tt // 8
    return 8 * (p + 1 if p % 2 == 0 else p + 2)


def _peer_expert_kernel(h_ref, ut_ref, v_ref, i1_ref, i2_ref, gate_ref, x_ref, gf_ref,
                        y_ref, slab_ref, acc_ref, *, n_chunks, pitch, final_norm):
    s = pl.program_id(1)
    tt = h_ref.shape[0]
    slabs_per_chunk = ut_ref.shape[1] // PEER_KEYS
    key_iota = lax.broadcasted_iota(jnp.int32, (PEER_KEYS, PEER_SEL), 0).astype(F32)

    def slab_rows(g):
        return pl.ds(pl.multiple_of(g * pitch, 8), tt)

    def token_rows(t):
        return pl.ds(t, PEER_KEYS, stride=pitch)

    @pl.when(s < n_chunks)
    def _():
        a = _dot(h_ref[...], ut_ref[...])
        for gl in range(slabs_per_chunk):
            slab_ref[slab_rows(s * slabs_per_chunk + gl), :] = a[:, gl * PEER_KEYS:(gl + 1) * PEER_KEYS]

    @pl.when(s == n_chunks - 1)
    def _():
        sub = lax.broadcasted_iota(jnp.int32, (PEER_GROUP, PEER_SEL), 0)

        def group(gi, c):
            base = pl.multiple_of(gi * PEER_GROUP, PEER_GROUP)
            hits, rows = [], []
            for u in range(PEER_GROUP):
                t = base + u
                pre = slab_ref[token_rows(t), :]
                hit2 = (key_iota == i2_ref[pl.ds(t, 1), :]).astype(BF16)
                hit1 = key_iota == i1_ref[pl.ds(t, 1), :]
                both = _dot(jnp.concatenate(_split2(pre), axis=0), hit2)
                picked = both[:PEER_KEYS] + both[PEER_KEYS:]
                rows.append(jnp.sum(jnp.where(hit1, picked, 0.0), axis=0, keepdims=True))
                hits.append((hit1, hit2))
            pre = _stack_rows(rows, sub)
            wv = gate_ref[pl.ds(base, PEER_GROUP), :] * (0.5 * pre * (1.0 + lax.erf(pre * (0.5 ** 0.5))))
            for u, (hit1, hit2) in enumerate(hits):
                w1 = jnp.where(hit1, wv[u:u + 1, :], 0.0).astype(BF16)
                slab_ref[token_rows(base + u), :] = _dot_nt(w1, hit2)
            return c

        lax.fori_loop(0, tt // PEER_GROUP, group, 0, unroll=PEER_GROUP_UNROLL)
        acc_ref[...] = jnp.zeros_like(acc_ref)

    @pl.when(s >= n_chunks)
    def _():
        c = s - n_chunks
        w = jnp.concatenate(
            [slab_ref[slab_rows(c * slabs_per_chunk + gl), :].astype(BF16) for gl in range(slabs_per_chunk)],
            axis=1)
        acc_ref[...] += _dot(w, v_ref[...])

    @pl.when(s == 2 * n_chunks - 1)
    def _():
        x2 = x_ref[...] + acc_ref[...]
        y_ref[...] = _rms(x2, gf_ref[...]) if final_norm else x2


def _peer_experts(h2, u_t, v_tab, i1, i2, gates, x1, g_final, final_norm, tt, chunk):
    n = h2.shape[0]
    n_exp = v_tab.shape[0]
    n_chunks = n_exp // chunk
    pitch = _slab_pitch(tt)
    row = pl.BlockSpec((tt, D_MODEL), lambda i, s: (i, 0))
    sel = pl.BlockSpec((tt, PEER_SEL), lambda i, s: (i, 0))
    return pl.pallas_call(
        functools.partial(_peer_expert_kernel, n_chunks=n_chunks, pitch=pitch, final_norm=final_norm),
        grid=(n // tt, 2 * n_chunks),
        in_specs=[
            row,
            pl.BlockSpec((D_MODEL, chunk), lambda i, s: (0, jnp.minimum(s, n_chunks - 1))),
            pl.BlockSpec((chunk, D_MODEL), lambda i, s: (jnp.maximum(s - n_chunks, 0), 0)),
            sel, sel, sel, row,
            pl.BlockSpec((1, D_MODEL), lambda i, s: (0, 0)),
        ],
        out_specs=row,
        out_shape=jax.ShapeDtypeStruct((n, D_MODEL), F32),
        scratch_shapes=[
            pltpu.VMEM((PEER_KEYS * pitch, PEER_KEYS), F32),
            pltpu.VMEM((tt, D_MODEL), F32),
        ],
        compiler_params=pltpu.CompilerParams(dimension_semantics=("parallel", "arbitrary")),
        name="peer_experts",
    )(h2, u_t, v_tab, i1, i2, gates, x1, g_final)


def _tile(n, target):
    return min(n, target)


def kernel(x_prompt, x_sample, cache_k, cache_v, state_hgrn, page_table, g_attn, w_in, hg_lb_logits,
           hg_norm, sb_bias, w_branch_a, w_branch_b, w_out, g_ffn, peer_wq, peer_subkeys, peer_u, peer_v,
           g_final):
    depth = w_in.shape[0]
    batch, seq, _ = x_prompt.shape
    nb, dec_seq, _ = x_sample.shape
    assert dec_seq == 1
    n_p = batch * seq
    xp = x_prompt.reshape(n_p, D_MODEL)
    xs = x_sample.reshape(nb, D_MODEL)
    g_fin = g_final.reshape(1, D_MODEL)
    outs = {k: [] for k in ("sp", "kp", "vp", "ss", "ks", "vs")}
    for l in range(depth):
        w_in_l = w_in[l].astype(BF16)
        wa, wb, wo = (w[l].astype(BF16) for w in (w_branch_a, w_branch_b, w_out))
        wq = peer_wq[l].astype(BF16)
        sk = peer_subkeys[l].astype(BF16)
        u_t = peer_u[l].astype(BF16).T
        v_tab = peer_v[l].astype(BF16)
        g_a = g_attn[l].reshape(1, D_MODEL)
        g_f = g_ffn[l].reshape(1, D_MODEL)
        ng = hg_norm[l].reshape(1, D_MODEL)
        bias_b = jnp.broadcast_to(sb_bias[l].astype(F32)[:, None], (N_HEADS, SB_BLOCK))
        last = l == depth - 1

        proj, k_new, v_new = _norm_proj(xp, g_a, w_in_l, _tile(n_p, 1024))
        o_a, s_p = _hgrn_prompt(proj, hg_lb_logits, ng, batch, seq, l, _tile(seq, 512))
        o_b = _sb_prompt(proj, sb_bias[l].astype(F32), batch, seq, _tile(seq, SB_TILE))
        x1, h2 = _merge(o_a, o_b, proj, xp, wa, wb, wo, g_f, _tile(n_p, 512))
        i1, i2, gates = _peer_route(h2, wq, sk, _tile(n_p, 256))
        xp = _peer_experts(h2, u_t, v_tab, i1, i2, gates, x1, g_fin, last, _tile(n_p, 256), 2048)
        outs["sp"].append(s_p)
        outs["kp"].append(k_new.reshape(batch, seq, N_HEADS, HEAD_DIM))
        outs["vp"].append(v_new.reshape(batch, seq, N_HEADS, HEAD_DIM))

        proj_s, k_new_s, v_new_s = _norm_proj(xs, g_a, w_in_l, nb)
        o_a_s, s_s = _hgrn_step(proj_s, hg_lb_logits, hg_norm[l], state_hgrn[l], l)
        o_b_s = _sb_decode(proj_s, bias_b, cache_k[l], cache_v[l], page_table, 8)
        x1_s, h2_s = _merge(o_a_s.reshape(nb, D_MODEL), o_b_s, proj_s, xs, wa, wb, wo, g_f, nb)
        pad = (-nb) % PEER_KEYS
        sel_s = _peer_route(jnp.pad(h2_s, ((0, pad), (0, 0))), wq, sk, PEER_KEYS)
        i1_s, i2_s, gates_s = (a[:nb] for a in sel_s)
        xs = _peer_experts(h2_s, u_t, v_tab, i1_s, i2_s, gates_s, x1_s, g_fin, last, nb, 2048)
        outs["ss"].append(s_s)
        outs["ks"].append(k_new_s.reshape(nb, 1, N_HEADS, HEAD_DIM))
        outs["vs"].append(v_new_s.reshape(nb, 1, N_HEADS, HEAD_DIM))

    y_prompt = xp.reshape(batch, seq, D_MODEL)
    y_sample = xs.reshape(nb, 1, D_MODEL)
    return (y_prompt, y_sample, jnp.stack(outs["sp"]), jnp.stack(outs["kp"]), jnp.stack(outs["vp"]),
            jnp.stack(outs["ss"]), jnp.stack(outs["ks"]), jnp.stack(outs["vs"]))
```

```python
import functools

import numpy as np
import jax
import jax.numpy as jnp
from jax import lax
from jax.experimental import pallas as pl
from jax.experimental.pallas import tpu as pltpu

F32 = jnp.float32
BF16 = jnp.bfloat16

D_MODEL = 1024
N_HEADS = 8
HEAD_DIM = 128
N_GROUPS = 9
HG_CHUNK = 64
SB_BLOCK = 128
SB_TILE = 1024
SB_SPAN = 256
PEER_GROUP = 8
PEER_GROUP_UNROLL = 2
PEER_KEYS = 128
PEER_TOPK = 16
PEER_SEL = N_HEADS * PEER_TOPK
RMS_EPS = 1e-6
NEG_INF = float("-inf")

G_HQ, G_HF, G_HI, G_HG, G_SQ, G_SK, G_SV, G_GA, G_GB = range(N_GROUPS)


def _dot(a, b):
    return jnp.dot(a, b, preferred_element_type=F32)


def _dot_nt(a, b):
    return lax.dot_general(a, b, (((1,), (1,)), ((), ())), preferred_element_type=F32)


def _dot_tn(a, b):
    return lax.dot_general(a, b, (((0,), (0,)), ((), ())), preferred_element_type=F32)


def _split2(x):
    hi = x.astype(BF16)
    lo = (x - hi.astype(F32)).astype(BF16)
    return hi, lo


def _split3(x):
    hi = x.astype(BF16)
    r = x - hi.astype(F32)
    mid = r.astype(BF16)
    lo = (r - mid.astype(F32)).astype(BF16)
    return hi, mid, lo


def _exact_dot_lhs01(m01, x):
    n = x.shape[1]
    parts = _dot(m01, jnp.concatenate(_split3(x), axis=1))
    return parts[:, :n] + parts[:, n:2 * n] + parts[:, 2 * n:]


def _exact_dot_rhs01(x, m01):
    hi, mid, lo = _split3(x)
    return _dot(hi, m01) + _dot(mid, m01) + _dot(lo, m01)


def _rms(x, g):
    return x * lax.rsqrt(jnp.mean(x * x, axis=-1, keepdims=True) + RMS_EPS) * g


def _silu(x):
    return x * jax.nn.sigmoid(x)


def _norm_proj_kernel(x_ref, g_ref, w_ref, o_ref, k_ref, v_ref, h_ref):
    j = pl.program_id(1)
    tm = x_ref.shape[0]

    @pl.when(j == 0)
    def _():
        h_ref[...] = _rms(x_ref[...], g_ref[...]).astype(BF16)

    out = _dot(h_ref[...], w_ref[...])
    o_ref[0] = out

    for group, ref in ((G_SK, k_ref), (G_SV, v_ref)):
        @pl.when(j == group)
        def _():
            for h in range(N_HEADS):
                ref[pl.ds(h, tm, stride=N_HEADS), :] = out[:, h * HEAD_DIM:(h + 1) * HEAD_DIM]


def _norm_proj(x, g, w, tm):
    n = x.shape[0]
    kv_spec = pl.BlockSpec((tm * N_HEADS, HEAD_DIM), lambda i, j: (i, 0))
    kv_shape = jax.ShapeDtypeStruct((n * N_HEADS, HEAD_DIM), F32)
    return pl.pallas_call(
        _norm_proj_kernel,
        grid=(n // tm, N_GROUPS),
        in_specs=[
            pl.BlockSpec((tm, D_MODEL), lambda i, j: (i, 0)),
            pl.BlockSpec((1, D_MODEL), lambda i, j: (0, 0)),
            pl.BlockSpec((D_MODEL, D_MODEL), lambda i, j: (0, j)),
        ],
        out_specs=[pl.BlockSpec((1, tm, D_MODEL), lambda i, j: (j, i, 0)), kv_spec, kv_spec],
        out_shape=[jax.ShapeDtypeStruct((N_GROUPS, n, D_MODEL), F32), kv_shape, kv_shape],
        scratch_shapes=[pltpu.VMEM((tm, D_MODEL), BF16)],
        compiler_params=pltpu.CompilerParams(dimension_semantics=("parallel", "arbitrary")),
        name="norm_proj",
    )(x, g, w)


def _hgrn_consts():
    c = HG_CHUNK
    t = np.arange(c)[:, None]
    j = np.arange(c)[None, :]
    sel = [(j <= t)]
    masks = []
    m = c // 2
    while m >= 1:
        mid = (t // (2 * m)) * (2 * m) + m
        upper = t >= mid
        d = np.where(upper, (j >= mid) & (j <= t), (j > t) & (j <= mid - 1))
        sel.append(d)
        s = j
        same = (t // (2 * m)) == (s // (2 * m))
        s_mid = (s // (2 * m)) * (2 * m) + m
        masks.append(same & upper & (s < s_mid))
        m //= 2
    sel.append(j > t)
    masks.append(t == j)
    sel = np.concatenate(sel, axis=0).astype(np.float32)
    masks = np.stack(masks).astype(np.float32)
    return jnp.asarray(sel, dtype=BF16), jnp.asarray(masks, dtype=F32)


def _lower_bound(lbl, layer):
    e = jnp.exp(lbl - jnp.max(lbl, axis=0, keepdims=True))
    return jnp.sum(e[: layer + 1], axis=0, keepdims=True) / jnp.sum(e, axis=0, keepdims=True)


def _hgrn_kernel(q_ref, f_ref, i_ref, g_ref, lbl_ref, ng_ref, sel_ref, mask_ref,
                 o_ref, s_ref, st_ref, *, n_sub, layer):
    c = pl.program_id(2)
    n_lvl = mask_ref.shape[0] - 1
    ch = HG_CHUNK

    @pl.when(c == 0)
    def _():
        st_ref[...] = jnp.zeros_like(st_ref)

    lb = _lower_bound(lbl_ref[...], layer)
    ng = ng_ref[...]
    sel = sel_ref[...]

    chunks = []
    for sc in range(n_sub):
        rows = slice(sc * ch, (sc + 1) * ch)
        f = lb + (1.0 - lb) * jax.nn.sigmoid(f_ref[0, rows, :])
        logf = jnp.log(f)
        k = 1.0 - f
        q = _silu(q_ref[0, rows, :])
        v = i_ref[0, rows, :].astype(BF16)
        ex = jnp.exp(_exact_dot_lhs01(sel, logf))
        scores = _dot_nt(q.astype(BF16), k.astype(BF16)) * mask_ref[n_lvl]
        for li in range(n_lvl):
            el = ex[(li + 1) * ch:(li + 2) * ch]
            scores += _dot_nt((q * el).astype(BF16), (k * el).astype(BF16)) * mask_ref[li]
        e_last = ex[(n_lvl + 1) * ch:(n_lvl + 2) * ch]
        chunks.append((
            (q * ex[0:ch]).astype(BF16),
            _dot(scores.astype(BF16), v),
            ex[ch - 1:ch],
            _dot_tn(v, (k * e_last).astype(BF16)),
        ))

    st = st_ref[...]
    for sc, (q_dec, o_intra, decay, kv) in enumerate(chunks):
        rows = slice(sc * ch, (sc + 1) * ch)
        o = o_intra + _dot_nt(q_dec, st.astype(BF16))
        st = st * decay + kv
        o = o * lax.rsqrt(jnp.mean(o * o, axis=-1, keepdims=True) + RMS_EPS)
        o_ref[rows, :] = (o * ng * _silu(g_ref[0, rows, :])).astype(o_ref.dtype)
    st_ref[...] = st

    @pl.when(c == pl.num_programs(2) - 1)
    def _():
        s_ref[0, 0] = st_ref[...].T


def _hgrn_prompt(proj, lb_logits, norm_g, batch, seq, layer, tc):
    sel, masks = _hgrn_consts()
    nc = seq // tc
    n = batch * seq

    def pmap(g):
        return lambda b, h, c: (g, b * nc + c, h)

    kern = functools.partial(_hgrn_kernel, n_sub=tc // HG_CHUNK, layer=layer)
    return pl.pallas_call(
        kern,
        grid=(batch, N_HEADS, nc),
        in_specs=[
            pl.BlockSpec((1, tc, HEAD_DIM), pmap(G_HQ)),
            pl.BlockSpec((1, tc, HEAD_DIM), pmap(G_HF)),
            pl.BlockSpec((1, tc, HEAD_DIM), pmap(G_HI)),
            pl.BlockSpec((1, tc, HEAD_DIM), pmap(G_HG)),
            pl.BlockSpec((lb_logits.shape[0], HEAD_DIM), lambda b, h, c: (0, h)),
            pl.BlockSpec((1, HEAD_DIM), lambda b, h, c: (0, h)),
            pl.BlockSpec(sel.shape, lambda b, h, c: (0, 0)),
            pl.BlockSpec(masks.shape, lambda b, h, c: (0, 0, 0)),
        ],
        out_specs=[
            pl.BlockSpec((tc, HEAD_DIM), lambda b, h, c: (b * nc + c, h)),
            pl.BlockSpec((1, 1, HEAD_DIM, HEAD_DIM), lambda b, h, c: (b, h, 0, 0)),
        ],
        out_shape=[
            jax.ShapeDtypeStruct((n, D_MODEL), BF16),
            jax.ShapeDtypeStruct((batch, N_HEADS, HEAD_DIM, HEAD_DIM), F32),
        ],
        scratch_shapes=[pltpu.VMEM((HEAD_DIM, HEAD_DIM), F32)],
        compiler_params=pltpu.CompilerParams(
            dimension_semantics=("parallel", "parallel", "arbitrary")),
        name="hgrn_prompt",
    )(proj, proj, proj, proj, lb_logits, norm_g, sel, masks)


def _hgrn_step_kernel(q_ref, f_ref, i_ref, g_ref, lbl_ref, ng_ref, s_ref, o_ref, sn_ref, *, layer):
    lb = _lower_bound(lbl_ref[...], layer)
    lb = lb[0]
    f = lb + (1.0 - lb) * jax.nn.sigmoid(f_ref[0, 0])
    q = _silu(q_ref[0, 0])
    v = i_ref[0, 0]
    g = g_ref[0, 0]
    ng = ng_ref[...]
    pad = jnp.zeros((HEAD_DIM - 2 * N_HEADS, HEAD_DIM), F32)
    cols = jnp.concatenate([f, q, pad], axis=0).T
    for h in range(N_HEADS):
        fcol = cols[:, h:h + 1]
        qcol = cols[:, N_HEADS + h:N_HEADS + h + 1]
        s_new = fcol * s_ref[0, h] + (1.0 - fcol) * v[h:h + 1, :]
        sn_ref[0, h] = s_new
        o = jnp.sum(qcol * s_new, axis=0, keepdims=True)
        o = o * lax.rsqrt(jnp.mean(o * o, axis=-1, keepdims=True) + RMS_EPS)
        o_ref[0, pl.ds(h, 1), :] = o * ng[h:h + 1, :] * _silu(g[h:h + 1, :])


def _hgrn_step(proj, lb_logits, norm_g, state, layer):
    nb = state.shape[0]
    proj4 = proj.reshape(N_GROUPS, nb, N_HEADS, HEAD_DIM)
    lbl = lb_logits.reshape(lb_logits.shape[0], N_HEADS, HEAD_DIM)
    ng = norm_g.reshape(N_HEADS, HEAD_DIM)

    def pmap(g):
        return lambda b: (g, b, 0, 0)

    blk = (1, 1, N_HEADS, HEAD_DIM)
    return pl.pallas_call(
        functools.partial(_hgrn_step_kernel, layer=layer),
        grid=(nb,),
        in_specs=[
            pl.BlockSpec(blk, pmap(G_HQ)),
            pl.BlockSpec(blk, pmap(G_HF)),
            pl.BlockSpec(blk, pmap(G_HI)),
            pl.BlockSpec(blk, pmap(G_HG)),
            pl.BlockSpec(lbl.shape, lambda b: (0, 0, 0)),
            pl.BlockSpec(ng.shape, lambda b: (0, 0)),
            pl.BlockSpec((1, N_HEADS, HEAD_DIM, HEAD_DIM), lambda b: (b, 0, 0, 0)),
        ],
        out_specs=[
            pl.BlockSpec((1, N_HEADS, HEAD_DIM), lambda b: (b, 0, 0)),
            pl.BlockSpec((1, N_HEADS, HEAD_DIM, HEAD_DIM), lambda b: (b, 0, 0, 0)),
        ],
        out_shape=[
            jax.ShapeDtypeStruct((nb, N_HEADS, HEAD_DIM), F32),
            jax.ShapeDtypeStruct(state.shape, F32),
        ],
        name="hgrn_step",
    )(proj4, proj4, proj4, proj4, lbl, ng, state)


def _later_sum_consts():
    j = np.arange(SB_BLOCK)[:, None]
    k = np.arange(SB_BLOCK)[None, :]
    u = np.concatenate([(j > k), np.ones((SB_BLOCK, SB_BLOCK), bool)], axis=1)
    return jnp.asarray(u.astype(np.float32), dtype=BF16)


def _log_beta_terms(z):
    l1p = jnp.log(1.0 + jnp.exp(-jnp.abs(z)))
    return jnp.minimum(z, 0.0) - l1p, -jnp.maximum(z, 0.0) - l1p


def _strictly_later_const(span):
    j = np.arange(span)[:, None]
    k = np.arange(span)[None, :]
    return jnp.asarray((j > k).astype(np.float32), dtype=BF16)


def _mask_last_block(x, visible):
    cut = x.shape[1] - SB_BLOCK
    last = jnp.where(visible, x[:, cut:], 0.0)
    return last if cut == 0 else jnp.concatenate([x[:, :cut], last], axis=1)


def _sb_rows(q, k, v, carry, acc, later01, bias, visible):
    blk = SB_BLOCK
    span = later01.shape[0]
    n_keys = k.shape[0]
    z = _dot_nt(q, k) + bias
    log_b = jnp.minimum(z, 0.0) - jnp.log(1.0 + jnp.exp(-jnp.abs(z)))
    log_keep = log_b - z
    if visible is not None:
        log_keep = _mask_last_block(log_keep, visible)
    ws = []
    for c0 in reversed(range(0, n_keys, span)):
        wd = min(span, n_keys - c0)
        lk = log_keep[:, c0:c0 + wd]
        u = later01 if wd == span else later01[:wd, :wd]
        later = _dot(jnp.concatenate(_split2(lk), axis=1), jnp.concatenate([u, u], axis=0))
        ws.insert(0, jnp.exp(log_b[:, c0:c0 + wd] + later + jnp.concatenate([carry] * (wd // blk), axis=1)))
        carry = carry + (later[:, 0:1] + lk[:, 0:1])
    w = jnp.concatenate(ws, axis=1)
    if visible is not None:
        w = _mask_last_block(w, visible)
    return carry, acc + _dot(w.astype(BF16), v)


def _sb_prompt_kernel(bias_ref, q_ref, k_ref, v_ref, uo_ref, o_ref, *, tile):
    h = pl.program_id(1)
    qi = pl.program_id(2)
    blk = SB_BLOCK
    nsub = tile // blk
    q = (q_ref[0] * (HEAD_DIM ** -0.5)).astype(BF16)
    bias = bias_ref[h]
    uo = uo_ref[...]
    visible = (lax.broadcasted_iota(jnp.int32, (blk, blk), 1)
               < lax.broadcasted_iota(jnp.int32, (blk, blk), 0))

    def keys(kt):
        rows = pl.ds(pl.multiple_of(kt * tile, tile), tile)
        return k_ref[0, rows, :].astype(BF16), v_ref[0, rows, :].astype(BF16)

    k, v = keys(qi)
    zeros = jnp.zeros((blk, blk), F32)
    parts = [_sb_rows(q[r * blk:(r + 1) * blk], k[:(r + 1) * blk], v[:(r + 1) * blk],
                      zeros, zeros, uo, bias, visible) for r in range(nsub)]
    carry = jnp.concatenate([p[0] for p in parts], axis=0)
    acc = jnp.concatenate([p[1] for p in parts], axis=0)

    def body(s, ca):
        k, v = keys(qi - 1 - s)
        return _sb_rows(q, k, v, ca[0], ca[1], uo, bias, None)

    carry, acc = lax.fori_loop(0, qi, body, (carry, acc))
    o_ref[...] = acc.astype(o_ref.dtype)


def _sb_prompt(proj, bias, batch, seq, tile):
    nq = seq // tile
    n = batch * seq
    uo = _strictly_later_const(min(tile, SB_SPAN))
    return pl.pallas_call(
        functools.partial(_sb_prompt_kernel, tile=tile),
        grid=(batch, N_HEADS, nq),
        in_specs=[
            pl.BlockSpec(memory_space=pltpu.SMEM),
            pl.BlockSpec((1, tile, HEAD_DIM), lambda b, h, i: (G_SQ, b * nq + i, h)),
            pl.BlockSpec((1, seq, HEAD_DIM), lambda b, h, i: (G_SK, b, h)),
            pl.BlockSpec((1, seq, HEAD_DIM), lambda b, h, i: (G_SV, b, h)),
            pl.BlockSpec(uo.shape, lambda b, h, i: (0, 0)),
        ],
        out_specs=pl.BlockSpec((tile, HEAD_DIM), lambda b, h, i: (b * nq + i, h)),
        out_shape=jax.ShapeDtypeStruct((n, D_MODEL), BF16),
        compiler_params=pltpu.CompilerParams(
            dimension_semantics=("parallel", "parallel", "arbitrary")),
        name="sb_prompt",
    )(bias, proj, proj, proj, uo)


def _sb_decode_kernel(pt_ref, q_ref, bias_ref, uo_ref, *refs, pages_per_step):
    k_refs = refs[:pages_per_step]
    v_refs = refs[pages_per_step:2 * pages_per_step]
    o_ref, carry_ref, acc_ref = refs[2 * pages_per_step:]
    s = pl.program_id(1)
    scale = HEAD_DIM ** -0.5
    blk = SB_BLOCK
    span = pages_per_step * blk
    head_of_row = lax.broadcasted_iota(jnp.int32, (N_HEADS, span), 0)

    @pl.when(s == 0)
    def _():
        carry_ref[...] = jnp.zeros_like(carry_ref)
        acc_ref[...] = jnp.zeros_like(acc_ref)

    def head_rows(refs, h):
        return jnp.concatenate(
            [ref[0, pl.ds(h, blk, stride=N_HEADS), :] for ref in refs], axis=0).astype(BF16)

    q = q_ref[0, 0].astype(BF16)
    z = jnp.zeros((N_HEADS, span), F32)
    for h in range(N_HEADS):
        z = jnp.where(head_of_row == h, _dot_nt(q, head_rows(k_refs, h)), z)
    z = z * scale + jnp.concatenate([bias_ref[...]] * pages_per_step, axis=1)
    log_b, log_keep = _log_beta_terms(z)
    lk_rows = jnp.concatenate([log_keep[:, r * blk:(r + 1) * blk] for r in range(pages_per_step)], axis=0)
    cs = _exact_dot_rhs01(lk_rows, uo_ref[...])
    carry = carry_ref[...]
    ws = []
    for r in range(pages_per_step):
        rows = slice(r * N_HEADS, (r + 1) * N_HEADS)
        ws.append(jnp.exp(log_b[:, r * blk:(r + 1) * blk] + cs[rows, :blk] + carry).astype(BF16))
        carry = carry + cs[rows, blk:]
    carry_ref[...] = carry
    w = jnp.concatenate(ws, axis=1)
    for h in range(N_HEADS):
        acc_ref[h] += _dot(w, head_rows(v_refs, h))

    @pl.when(s == pl.num_programs(1) - 1)
    def _():
        out = jnp.zeros((N_HEADS, HEAD_DIM), F32)
        for h in range(N_HEADS):
            out = jnp.where(head_of_row[:, :HEAD_DIM] == h, acc_ref[h], out)
        o_ref[0] = out


def _sb_decode(proj, bias_b, cache_k, cache_v, page_table, pages_per_step):
    nb, n_pages = page_table.shape
    n_pool, page = cache_k.shape[0], cache_k.shape[1]
    assert page == SB_BLOCK
    ck = cache_k.reshape(n_pool, page * N_HEADS, HEAD_DIM)
    cv = cache_v.reshape(n_pool, page * N_HEADS, HEAD_DIM)
    proj4 = proj.reshape(N_GROUPS, nb, N_HEADS, HEAD_DIM)
    uo = _later_sum_consts()
    pt = page_table.reshape(-1)

    def page_map(r):
        return lambda b, s, pt: (pt[b * n_pages + n_pages - 1 - (s * pages_per_step + r)], 0, 0)

    def page_specs():
        return [pl.BlockSpec((1, page * N_HEADS, HEAD_DIM), page_map(r)) for r in range(pages_per_step)]

    grid_spec = pltpu.PrefetchScalarGridSpec(
        num_scalar_prefetch=1,
        grid=(nb, n_pages // pages_per_step),
        in_specs=[
            pl.BlockSpec((1, 1, N_HEADS, HEAD_DIM), lambda b, s, pt: (G_SQ, b, 0, 0)),
            pl.BlockSpec(bias_b.shape, lambda b, s, pt: (0, 0)),
            pl.BlockSpec(uo.shape, lambda b, s, pt: (0, 0)),
        ] + page_specs() + page_specs(),
        out_specs=pl.BlockSpec((1, N_HEADS, HEAD_DIM), lambda b, s, pt: (b, 0, 0)),
        scratch_shapes=[pltpu.VMEM((N_HEADS, SB_BLOCK), F32),
                        pltpu.VMEM((N_HEADS, N_HEADS, HEAD_DIM), F32)],
    )
    out = pl.pallas_call(
        functools.partial(_sb_decode_kernel, pages_per_step=pages_per_step),
        grid_spec=grid_spec,
        out_shape=jax.ShapeDtypeStruct((nb, N_HEADS, HEAD_DIM), F32),
        compiler_params=pltpu.CompilerParams(dimension_semantics=("parallel", "arbitrary")),
        name="sb_decode",
    )(pt, proj4, bias_b, uo, *([ck] * pages_per_step), *([cv] * pages_per_step))
    return out.reshape(nb, D_MODEL)


def _merge_kernel(oa_ref, ob_ref, ga_ref, gb_ref, x_ref, wa_ref, wb_ref, wo_ref, gf_ref,
                  x1_ref, h2_ref):
    a = _dot(oa_ref[...].astype(BF16), wa_ref[...])
    b = _dot(ob_ref[...].astype(BF16), wb_ref[...])
    m = jax.nn.sigmoid(ga_ref[0]) * a + jax.nn.sigmoid(gb_ref[0]) * b
    x1 = x_ref[...] + _dot(m.astype(BF16), wo_ref[...])
    x1_ref[...] = x1
    h2_ref[...] = _rms(x1, gf_ref[...]).astype(BF16)


def _merge(o_a, o_b, proj, x, wa, wb, wo, g_ffn, tm):
    n = x.shape[0]
    row = pl.BlockSpec((tm, D_MODEL), lambda i: (i, 0))
    wspec = pl.BlockSpec((D_MODEL, D_MODEL), lambda i: (0, 0))
    return pl.pallas_call(
        _merge_kernel,
        grid=(n // tm,),
        in_specs=[
            row, row,
            pl.BlockSpec((1, tm, D_MODEL), lambda i: (G_GA, i, 0)),
            pl.BlockSpec((1, tm, D_MODEL), lambda i: (G_GB, i, 0)),
            row, wspec, wspec, wspec,
            pl.BlockSpec((1, D_MODEL), lambda i: (0, 0)),
        ],
        out_specs=[row, row],
        out_shape=[jax.ShapeDtypeStruct((n, D_MODEL), F32), jax.ShapeDtypeStruct((n, D_MODEL), BF16)],
        compiler_params=pltpu.CompilerParams(dimension_semantics=("parallel",)),
        name="merge",
    )(o_a, o_b, proj, proj, x, wa, wb, wo, g_ffn)


def _top_rows(vals, order, payload, n):
    out_v, out_p = [], []
    for _ in range(n):
        m = jnp.max(vals, axis=0, keepdims=True)
        first = jnp.min(jnp.where(vals == m, order, 1e9), axis=0, keepdims=True)
        hit = order == first
        out_v.append(m)
        if payload is None:
            out_p.append(first)
        else:
            out_p.append(jnp.max(jnp.where(hit, payload, -1.0), axis=0, keepdims=True))
        vals = jnp.where(hit, NEG_INF, vals)
    return out_v, out_p


def _stack_rows(rows, sub):
    out = jnp.zeros(sub.shape, F32)
    for r, row in enumerate(rows):
        out = jnp.where(sub == r, row, out)
    return out


def _peer_route_kernel(h_ref, wq_ref, sk_ref, i1_ref, i2_ref, gate_ref):
    t = h_ref.shape[0]
    q = _dot(h_ref[...], wq_ref[...]).astype(BF16)
    key_iota = lax.broadcasted_iota(jnp.int32, (PEER_KEYS, t), 0).astype(F32)
    sub16 = lax.broadcasted_iota(jnp.int32, (PEER_TOPK, t), 0)
    sub16f = sub16.astype(F32)
    sub8 = lax.broadcasted_iota(jnp.int32, (8, t), 0)
    sub8f = sub8.astype(F32)
    e_rows, g_rows = [], []
    for h in range(N_HEADS):
        top_v, top_i = [], []
        for p in range(2):
            c0 = (h * 2 + p) * PEER_KEYS
            s = _dot_nt(sk_ref[p], q[:, c0:c0 + PEER_KEYS])
            v, i = _top_rows(s, key_iota, None, PEER_TOPK)
            top_v.append(v)
            top_i.append(i)
        v2 = (_stack_rows(top_v[1][:8], sub8), _stack_rows(top_v[1], sub16))
        i2 = (_stack_rows(top_i[1][:8], sub8), _stack_rows(top_i[1], sub16))
        cv, co, ce = [], [], []
        for a in range(PEER_TOPK):
            nb = PEER_TOPK // (a + 1)
            wide = int(a == 0)
            sub, subf = (sub16, sub16f) if wide else (sub8, sub8f)
            cv.append(jnp.where(sub < nb, top_v[0][a] + v2[wide], NEG_INF))
            co.append(subf + float(a * PEER_TOPK))
            ce.append(top_i[0][a] * float(PEER_KEYS) + i2[wide])
        best, experts = _top_rows(jnp.concatenate(cv, axis=0), jnp.concatenate(co, axis=0),
                                  jnp.concatenate(ce, axis=0), PEER_TOPK)
        top = best[0]
        best = _stack_rows(best, sub16)
        ex = jnp.exp(best - top)
        g_rows.append(ex / jnp.sum(ex, axis=0, keepdims=True))
        e_rows.append(_stack_rows(experts, sub16))
    experts = jnp.concatenate(e_rows, axis=0)
    gates = jnp.concatenate(g_rows, axis=0)
    key1 = jnp.floor(experts * (1.0 / PEER_KEYS))
    i1_ref[...] = key1.T
    i2_ref[...] = (experts - key1 * float(PEER_KEYS)).T
    gate_ref[...] = gates.T


def _peer_route(h2, wq, sub_keys, tt):
    n = h2.shape[0]
    sel = pl.BlockSpec((tt, PEER_SEL), lambda i: (i, 0))
    shp = jax.ShapeDtypeStruct((n, PEER_SEL), F32)
    return pl.pallas_call(
        _peer_route_kernel,
        grid=(n // tt,),
        in_specs=[
            pl.BlockSpec((tt, D_MODEL), lambda i: (i, 0)),
            pl.BlockSpec(wq.shape, lambda i: (0, 0)),
            pl.BlockSpec(sub_keys.shape, lambda i: (0, 0, 0)),
        ],
        out_specs=[sel, sel, sel],
        out_shape=[shp, shp, shp],
        compiler_params=pltpu.CompilerParams(dimension_semantics=("parallel",)),
        name="peer_route",
    )(h2, wq, sub_keys)


def _slab_pitch(tt):
    p = tt // 8
    return 8 * (p + 1 if p % 2 == 0 else p + 2)


def _peer_expert_kernel(h_ref, ut_ref, v_ref, i1_ref, i2_ref, gate_ref, x_ref, gf_ref,
                        y_ref, slab_ref, acc_ref, *, n_chunks, pitch, final_norm):
    s = pl.program_id(1)
    tt = h_ref.shape[0]
    slabs_per_chunk = ut_ref.shape[1] // PEER_KEYS
    key_iota = lax.broadcasted_iota(jnp.int32, (PEER_KEYS, PEER_SEL), 0).astype(F32)

    def slab_rows(g):
        return pl.ds(pl.multiple_of(g * pitch, 8), tt)

    def token_rows(t):
        return pl.ds(t, PEER_KEYS, stride=pitch)

    @pl.when(s < n_chunks)
    def _():
        a = _dot(h_ref[...], ut_ref[...])
        for gl in range(slabs_per_chunk):
            slab_ref[slab_rows(s * slabs_per_chunk + gl), :] = a[:, gl * PEER_KEYS:(gl + 1) * PEER_KEYS]

    @pl.when(s == n_chunks - 1)
    def _():
        sub = lax.broadcasted_iota(jnp.int32, (PEER_GROUP, PEER_SEL), 0)

        def group_base(gi):
            base = gi * PEER_GROUP
            return base if isinstance(base, int) else pl.multiple_of(base, PEER_GROUP)

        def hits(t):
            return (key_iota == i1_ref[pl.ds(t, 1), :],
                    (key_iota == i2_ref[pl.ds(t, 1), :]).astype(BF16))

        def select(gi):
            base = group_base(gi)
            rows = []
            for u in range(PEER_GROUP):
                t = base + u
                hit1, hit2 = hits(t)
                pre = slab_ref[token_rows(t), :]
                col = jnp.broadcast_to(i2_ref[pl.ds(t, 1), :].astype(jnp.int32), (PEER_KEYS, PEER_SEL))
                picked = jnp.take_along_axis(pre, col, axis=1)
                rows.append(jnp.sum(jnp.where(hit1, picked, 0.0), axis=0, keepdims=True))
            return _stack_rows(rows, sub)

        def scatter(gi, pre):
            base = group_base(gi)
            wv = gate_ref[pl.ds(base, PEER_GROUP), :] * (0.5 * pre * (1.0 + lax.erf(pre * (0.5 ** 0.5))))
            for u in range(PEER_GROUP):
                hit1, hit2 = hits(base + u)
                w1 = jnp.where(hit1, wv[u:u + 1, :], 0.0).astype(BF16)
                slab_ref[token_rows(base + u), :] = _dot_nt(w1, hit2)

        def step(gi, pre_prev):
            pre = select(gi)
            scatter(gi - 1, pre_prev)
            return pre

        n_groups = tt // PEER_GROUP
        pre_last = lax.fori_loop(1, n_groups, step, select(0), unroll=PEER_GROUP_UNROLL)
        scatter(n_groups - 1, pre_last)
        acc_ref[...] = jnp.zeros_like(acc_ref)

    @pl.when(s >= n_chunks)
    def _():
        c = s - n_chunks
        w = jnp.concatenate(
            [slab_ref[slab_rows(c * slabs_per_chunk + gl), :].astype(BF16) for gl in range(slabs_per_chunk)],
            axis=1)
        acc_ref[...] += _dot(w, v_ref[...])

    @pl.when(s == 2 * n_chunks - 1)
    def _():
        x2 = x_ref[...] + acc_ref[...]
        y_ref[...] = _rms(x2, gf_ref[...]) if final_norm else x2


def _peer_experts(h2, u_t, v_tab, i1, i2, gates, x1, g_final, final_norm, tt, chunk):
    n = h2.shape[0]
    n_exp = v_tab.shape[0]
    n_chunks = n_exp // chunk
    pitch = _slab_pitch(tt)
    row = pl.BlockSpec((tt, D_MODEL), lambda i, s: (i, 0))
    sel = pl.BlockSpec((tt, PEER_SEL), lambda i, s: (i, 0))
    return pl.pallas_call(
        functools.partial(_peer_expert_kernel, n_chunks=n_chunks, pitch=pitch, final_norm=final_norm),
        grid=(n // tt, 2 * n_chunks),
        in_specs=[
            row,
            pl.BlockSpec((D_MODEL, chunk), lambda i, s: (0, jnp.minimum(s, n_chunks - 1))),
            pl.BlockSpec((chunk, D_MODEL), lambda i, s: (jnp.maximum(s - n_chunks, 0), 0)),
            sel, sel, sel, row,
            pl.BlockSpec((1, D_MODEL), lambda i, s: (0, 0)),
        ],
        out_specs=row,
        out_shape=jax.ShapeDtypeStruct((n, D_MODEL), F32),
        scratch_shapes=[
            pltpu.VMEM((PEER_KEYS * pitch, PEER_KEYS), F32),
            pltpu.VMEM((tt, D_MODEL), F32),
        ],
        compiler_params=pltpu.CompilerParams(dimension_semantics=("parallel", "arbitrary")),
        name="peer_experts",
    )(h2, u_t, v_tab, i1, i2, gates, x1, g_final)


def _tile(n, target):
    return min(n, target)


def kernel(x_prompt, x_sample, cache_k, cache_v, state_hgrn, page_table, g_attn, w_in, hg_lb_logits,
           hg_norm, sb_bias, w_branch_a, w_branch_b, w_out, g_ffn, peer_wq, peer_subkeys, peer_u, peer_v,
           g_final):
    depth = w_in.shape[0]
    batch, seq, _ = x_prompt.shape
    nb, dec_seq, _ = x_sample.shape
    assert dec_seq == 1
    n_p = batch * seq
    xp = x_prompt.reshape(n_p, D_MODEL)
    xs = x_sample.reshape(nb, D_MODEL)
    g_fin = g_final.reshape(1, D_MODEL)
    outs = {k: [] for k in ("sp", "kp", "vp", "ss", "ks", "vs")}
    for l in range(depth):
        w_in_l = w_in[l].astype(BF16)
        wa, wb, wo = (w[l].astype(BF16) for w in (w_branch_a, w_branch_b, w_out))
        wq = peer_wq[l].astype(BF16)
        sk = peer_subkeys[l].astype(BF16)
        u_t = peer_u[l].astype(BF16).T
        v_tab = peer_v[l].astype(BF16)
        g_a = g_attn[l].reshape(1, D_MODEL)
        g_f = g_ffn[l].reshape(1, D_MODEL)
        ng = hg_norm[l].reshape(1, D_MODEL)
        bias_b = jnp.broadcast_to(sb_bias[l].astype(F32)[:, None], (N_HEADS, SB_BLOCK))
        last = l == depth - 1

        proj, k_new, v_new = _norm_proj(xp, g_a, w_in_l, _tile(n_p, 1024))
        o_a, s_p = _hgrn_prompt(proj, hg_lb_logits, ng, batch, seq, l, _tile(seq, 512))
        o_b = _sb_prompt(proj, sb_bias[l].astype(F32), batch, seq, _tile(seq, SB_TILE))
        x1, h2 = _merge(o_a, o_b, proj, xp, wa, wb, wo, g_f, _tile(n_p, 512))
        i1, i2, gates = _peer_route(h2, wq, sk, _tile(n_p, 256))
        xp = _peer_experts(h2, u_t, v_tab, i1, i2, gates, x1, g_fin, last, _tile(n_p, 512), 1024)
        outs["sp"].append(s_p)
        outs["kp"].append(k_new.reshape(batch, seq, N_HEADS, HEAD_DIM))
        outs["vp"].append(v_new.reshape(batch, seq, N_HEADS, HEAD_DIM))

        proj_s, k_new_s, v_new_s = _norm_proj(xs, g_a, w_in_l, nb)
        o_a_s, s_s = _hgrn_step(proj_s, hg_lb_logits, hg_norm[l], state_hgrn[l], l)
        o_b_s = _sb_decode(proj_s, bias_b, cache_k[l], cache_v[l], page_table, 8)
        x1_s, h2_s = _merge(o_a_s.reshape(nb, D_MODEL), o_b_s, proj_s, xs, wa, wb, wo, g_f, nb)
        pad = (-nb) % PEER_KEYS
        sel_s = _peer_route(jnp.pad(h2_s, ((0, pad), (0, 0))), wq, sk, PEER_KEYS)
        i1_s, i2_s, gates_s = (a[:nb] for a in sel_s)
        xs = _peer_experts(h2_s, u_t, v_tab, i1_s, i2_s, gates_s, x1_s, g_fin, last, nb, 2048)
        outs["ss"].append(s_s)
        outs["ks"].append(k_new_s.reshape(nb, 1, N_HEADS, HEAD_DIM))
        outs["vs"].append(v_new_s.reshape(nb, 1, N_HEADS, HEAD_DIM))

    y_prompt = xp.reshape(batch, seq, D_MODEL)
    y_sample = xs.reshape(nb, 1, D_MODEL)
    return (y_prompt, y_sample, jnp.stack(outs["sp"]), jnp.stack(outs["kp"]), jnp.stack(outs["vp"]),
            jnp.stack(outs["ss"]), jnp.stack(outs["ks"]), jnp.stack(outs["vs"]))
```

```python
import functools

import numpy as np
import jax
import jax.numpy as jnp
from jax import lax
from jax.experimental import pallas as pl
from jax.experimental.pallas import tpu as pltpu

F32 = jnp.float32
BF16 = jnp.bfloat16

D_MODEL = 1024
N_HEADS = 8
HEAD_DIM = 128
N_GROUPS = 9
HG_CHUNK = 64
SB_BLOCK = 128
SB_TILE = 1024
SB_SPAN = 256
PEER_GROUP = 8
PEER_GROUP_UNROLL = 2
PEER_KEYS = 128
PEER_TOPK = 16
PEER_SEL = N_HEADS * PEER_TOPK
RMS_EPS = 1e-6
NEG_INF = float("-inf")
LOG2E = 1.4426950408889634

G_HQ, G_HF, G_HI, G_HG, G_SQ, G_SK, G_SV, G_GA, G_GB = range(N_GROUPS)


def _dot(a, b):
    return jnp.dot(a, b, preferred_element_type=F32)


def _dot_nt(a, b):
    return lax.dot_general(a, b, (((1,), (1,)), ((), ())), preferred_element_type=F32)


def _dot_tn(a, b):
    return lax.dot_general(a, b, (((0,), (0,)), ((), ())), preferred_element_type=F32)


def _split2(x):
    hi = x.astype(BF16)
    lo = (x - hi.astype(F32)).astype(BF16)
    return hi, lo


def _split3(x):
    hi = x.astype(BF16)
    r = x - hi.astype(F32)
    mid = r.astype(BF16)
    lo = (r - mid.astype(F32)).astype(BF16)
    return hi, mid, lo


def _exact_dot_lhs01(m01, x):
    n = x.shape[1]
    parts = _dot(m01, jnp.concatenate(_split3(x), axis=1))
    return parts[:, :n] + parts[:, n:2 * n] + parts[:, 2 * n:]


def _exact_dot_rhs01(x, m01):
    hi, mid, lo = _split3(x)
    return _dot(hi, m01) + _dot(mid, m01) + _dot(lo, m01)


def _rms(x, g):
    return x * lax.rsqrt(jnp.mean(x * x, axis=-1, keepdims=True) + RMS_EPS) * g


def _silu(x):
    return x * jax.nn.sigmoid(x)


def _norm_proj_kernel(x_ref, g_ref, w_ref, o_ref, k_ref, v_ref, h_ref):
    j = pl.program_id(1)
    tm = x_ref.shape[0]

    @pl.when(j == 0)
    def _():
        h_ref[...] = _rms(x_ref[...], g_ref[...]).astype(BF16)

    out = _dot(h_ref[...], w_ref[...])
    o_ref[0] = out

    for group, ref in ((G_SK, k_ref), (G_SV, v_ref)):
        @pl.when(j == group)
        def _():
            for h in range(N_HEADS):
                ref[pl.ds(h, tm, stride=N_HEADS), :] = out[:, h * HEAD_DIM:(h + 1) * HEAD_DIM]


def _norm_proj(x, g, w, tm):
    n = x.shape[0]
    kv_spec = pl.BlockSpec((tm * N_HEADS, HEAD_DIM), lambda i, j: (i, 0))
    kv_shape = jax.ShapeDtypeStruct((n * N_HEADS, HEAD_DIM), F32)
    return pl.pallas_call(
        _norm_proj_kernel,
        grid=(n // tm, N_GROUPS),
        in_specs=[
            pl.BlockSpec((tm, D_MODEL), lambda i, j: (i, 0)),
            pl.BlockSpec((1, D_MODEL), lambda i, j: (0, 0)),
            pl.BlockSpec((D_MODEL, D_MODEL), lambda i, j: (0, j)),
        ],
        out_specs=[pl.BlockSpec((1, tm, D_MODEL), lambda i, j: (j, i, 0)), kv_spec, kv_spec],
        out_shape=[jax.ShapeDtypeStruct((N_GROUPS, n, D_MODEL), F32), kv_shape, kv_shape],
        scratch_shapes=[pltpu.VMEM((tm, D_MODEL), BF16)],
        compiler_params=pltpu.CompilerParams(dimension_semantics=("parallel", "arbitrary")),
        name="norm_proj",
    )(x, g, w)


def _hgrn_consts():
    c = HG_CHUNK
    t = np.arange(c)[:, None]
    j = np.arange(c)[None, :]
    sel = [(j <= t)]
    masks = []
    m = c // 2
    while m >= 1:
        mid = (t // (2 * m)) * (2 * m) + m
        upper = t >= mid
        d = np.where(upper, (j >= mid) & (j <= t), (j > t) & (j <= mid - 1))
        sel.append(d)
        s = j
        same = (t // (2 * m)) == (s // (2 * m))
        s_mid = (s // (2 * m)) * (2 * m) + m
        masks.append(same & upper & (s < s_mid))
        m //= 2
    sel.append(j > t)
    masks.append(t == j)
    sel = np.concatenate(sel, axis=0).astype(np.float32)
    masks = np.stack(masks).astype(np.float32)
    return jnp.asarray(sel, dtype=BF16), jnp.asarray(masks, dtype=F32)


def _lower_bound(lbl, layer):
    e = jnp.exp(lbl - jnp.max(lbl, axis=0, keepdims=True))
    return jnp.sum(e[: layer + 1], axis=0, keepdims=True) / jnp.sum(e, axis=0, keepdims=True)


def _hgrn_kernel(q_ref, f_ref, i_ref, g_ref, lbl_ref, ng_ref, sel_ref, mask_ref,
                 o_ref, s_ref, st_ref, *, n_sub, layer):
    c = pl.program_id(2)
    n_lvl = mask_ref.shape[0] - 1
    ch = HG_CHUNK

    @pl.when(c == 0)
    def _():
        st_ref[...] = jnp.zeros_like(st_ref)

    lb = _lower_bound(lbl_ref[...], layer)
    ng = ng_ref[...]
    sel = sel_ref[...]

    chunks = []
    for sc in range(n_sub):
        rows = slice(sc * ch, (sc + 1) * ch)
        f = lb + (1.0 - lb) * jax.nn.sigmoid(f_ref[0, rows, :])
        logf = jnp.log(f)
        k = 1.0 - f
        q = _silu(q_ref[0, rows, :])
        v = i_ref[0, rows, :].astype(BF16)
        ex = jnp.exp(_exact_dot_lhs01(sel, logf))
        scores = _dot_nt(q.astype(BF16), k.astype(BF16)) * mask_ref[n_lvl]
        for li in range(n_lvl):
            el = ex[(li + 1) * ch:(li + 2) * ch]
            scores += _dot_nt((q * el).astype(BF16), (k * el).astype(BF16)) * mask_ref[li]
        e_last = ex[(n_lvl + 1) * ch:(n_lvl + 2) * ch]
        chunks.append((
            (q * ex[0:ch]).astype(BF16),
            _dot(scores.astype(BF16), v),
            ex[ch - 1:ch],
            _dot_tn(v, (k * e_last).astype(BF16)),
        ))

    st = st_ref[...]
    for sc, (q_dec, o_intra, decay, kv) in enumerate(chunks):
        rows = slice(sc * ch, (sc + 1) * ch)
        o = o_intra + _dot_nt(q_dec, st.astype(BF16))
        st = st * decay + kv
        o = o * lax.rsqrt(jnp.mean(o * o, axis=-1, keepdims=True) + RMS_EPS)
        o_ref[rows, :] = (o * ng * _silu(g_ref[0, rows, :])).astype(o_ref.dtype)
    st_ref[...] = st

    @pl.when(c == pl.num_programs(2) - 1)
    def _():
        s_ref[0, 0] = st_ref[...].T


def _hgrn_prompt(proj, lb_logits, norm_g, batch, seq, layer, tc):
    sel, masks = _hgrn_consts()
    nc = seq // tc
    n = batch * seq

    def pmap(g):
        return lambda b, h, c: (g, b * nc + c, h)

    kern = functools.partial(_hgrn_kernel, n_sub=tc // HG_CHUNK, layer=layer)
    return pl.pallas_call(
        kern,
        grid=(batch, N_HEADS, nc),
        in_specs=[
            pl.BlockSpec((1, tc, HEAD_DIM), pmap(G_HQ)),
            pl.BlockSpec((1, tc, HEAD_DIM), pmap(G_HF)),
            pl.BlockSpec((1, tc, HEAD_DIM), pmap(G_HI)),
            pl.BlockSpec((1, tc, HEAD_DIM), pmap(G_HG)),
            pl.BlockSpec((lb_logits.shape[0], HEAD_DIM), lambda b, h, c: (0, h)),
            pl.BlockSpec((1, HEAD_DIM), lambda b, h, c: (0, h)),
            pl.BlockSpec(sel.shape, lambda b, h, c: (0, 0)),
            pl.BlockSpec(masks.shape, lambda b, h, c: (0, 0, 0)),
        ],
        out_specs=[
            pl.BlockSpec((tc, HEAD_DIM), lambda b, h, c: (b * nc + c, h)),
            pl.BlockSpec((1, 1, HEAD_DIM, HEAD_DIM), lambda b, h, c: (b, h, 0, 0)),
        ],
        out_shape=[
            jax.ShapeDtypeStruct((n, D_MODEL), BF16),
            jax.ShapeDtypeStruct((batch, N_HEADS, HEAD_DIM, HEAD_DIM), F32),
        ],
        scratch_shapes=[pltpu.VMEM((HEAD_DIM, HEAD_DIM), F32)],
        compiler_params=pltpu.CompilerParams(
            dimension_semantics=("parallel", "parallel", "arbitrary")),
        name="hgrn_prompt",
    )(proj, proj, proj, proj, lb_logits, norm_g, sel, masks)


def _hgrn_step_kernel(q_ref, f_ref, i_ref, g_ref, lbl_ref, ng_ref, s_ref, o_ref, sn_ref, *, layer):
    lb = _lower_bound(lbl_ref[...], layer)
    lb = lb[0]
    f = lb + (1.0 - lb) * jax.nn.sigmoid(f_ref[0, 0])
    q = _silu(q_ref[0, 0])
    v = i_ref[0, 0]
    g = g_ref[0, 0]
    ng = ng_ref[...]
    pad = jnp.zeros((HEAD_DIM - 2 * N_HEADS, HEAD_DIM), F32)
    cols = jnp.concatenate([f, q, pad], axis=0).T
    for h in range(N_HEADS):
        fcol = cols[:, h:h + 1]
        qcol = cols[:, N_HEADS + h:N_HEADS + h + 1]
        s_new = fcol * s_ref[0, h] + (1.0 - fcol) * v[h:h + 1, :]
        sn_ref[0, h] = s_new
        o = jnp.sum(qcol * s_new, axis=0, keepdims=True)
        o = o * lax.rsqrt(jnp.mean(o * o, axis=-1, keepdims=True) + RMS_EPS)
        o_ref[0, pl.ds(h, 1), :] = o * ng[h:h + 1, :] * _silu(g[h:h + 1, :])


def _hgrn_step(proj, lb_logits, norm_g, state, layer):
    nb = state.shape[0]
    proj4 = proj.reshape(N_GROUPS, nb, N_HEADS, HEAD_DIM)
    lbl = lb_logits.reshape(lb_logits.shape[0], N_HEADS, HEAD_DIM)
    ng = norm_g.reshape(N_HEADS, HEAD_DIM)

    def pmap(g):
        return lambda b: (g, b, 0, 0)

    blk = (1, 1, N_HEADS, HEAD_DIM)
    return pl.pallas_call(
        functools.partial(_hgrn_step_kernel, layer=layer),
        grid=(nb,),
        in_specs=[
            pl.BlockSpec(blk, pmap(G_HQ)),
            pl.BlockSpec(blk, pmap(G_HF)),
            pl.BlockSpec(blk, pmap(G_HI)),
            pl.BlockSpec(blk, pmap(G_HG)),
            pl.BlockSpec(lbl.shape, lambda b: (0, 0, 0)),
            pl.BlockSpec(ng.shape, lambda b: (0, 0)),
            pl.BlockSpec((1, N_HEADS, HEAD_DIM, HEAD_DIM), lambda b: (b, 0, 0, 0)),
        ],
        out_specs=[
            pl.BlockSpec((1, N_HEADS, HEAD_DIM), lambda b: (b, 0, 0)),
            pl.BlockSpec((1, N_HEADS, HEAD_DIM, HEAD_DIM), lambda b: (b, 0, 0, 0)),
        ],
        out_shape=[
            jax.ShapeDtypeStruct((nb, N_HEADS, HEAD_DIM), F32),
            jax.ShapeDtypeStruct(state.shape, F32),
        ],
        name="hgrn_step",
    )(proj4, proj4, proj4, proj4, lbl, ng, state)


def _later_sum_consts():
    j = np.arange(SB_BLOCK)[:, None]
    k = np.arange(SB_BLOCK)[None, :]
    u = np.concatenate([(j > k), np.ones((SB_BLOCK, SB_BLOCK), bool)], axis=1)
    return jnp.asarray(u.astype(np.float32), dtype=BF16)


def _log_beta_terms(z):
    l1p = jnp.log(1.0 + jnp.exp(-jnp.abs(z)))
    return jnp.minimum(z, 0.0) - l1p, -jnp.maximum(z, 0.0) - l1p


def _strictly_later_const(span):
    j = np.arange(span)[:, None]
    k = np.arange(span)[None, :]
    return jnp.asarray((j > k).astype(np.float32), dtype=BF16)


def _mask_last_block(x, visible):
    cut = x.shape[1] - SB_BLOCK
    last = jnp.where(visible, x[:, cut:], 0.0)
    return last if cut == 0 else jnp.concatenate([x[:, :cut], last], axis=1)


def _sb_rows(q, k, v, carry, acc, later01, bias, visible):
    blk = SB_BLOCK
    span = later01.shape[0]
    n_keys = k.shape[0]
    z = _dot_nt(q, k) + bias
    neg_abs = lax.bitcast_convert_type(lax.bitcast_convert_type(z, jnp.uint32) | jnp.uint32(0x80000000), F32)
    log_b = jnp.minimum(z, 0.0) - jnp.log2(1.0 + jnp.exp2(neg_abs))
    log_keep = log_b - z
    if visible is not None:
        log_keep = _mask_last_block(log_keep, visible)
    ws = []
    for c0 in reversed(range(0, n_keys, span)):
        wd = min(span, n_keys - c0)
        lk = log_keep[:, c0:c0 + wd]
        u = later01 if wd == span else later01[:wd, :wd]
        later = _dot(lk.astype(BF16), u)
        ws.insert(0, jnp.exp2(log_b[:, c0:c0 + wd] + later + jnp.concatenate([carry] * (wd // blk), axis=1)))
        carry = carry + jnp.sum(lk, axis=1, keepdims=True)
    w = jnp.concatenate(ws, axis=1)
    if visible is not None:
        w = _mask_last_block(w, visible)
    return carry, acc + _dot(w.astype(BF16), v)


def _sb_prompt_kernel(bias_ref, q_ref, k_ref, v_ref, uo_ref, o_ref, *, tile):
    h = pl.program_id(1)
    qi = pl.program_id(2)
    blk = SB_BLOCK
    nsub = tile // blk
    q = (q_ref[0] * (LOG2E * HEAD_DIM ** -0.5)).astype(BF16)
    bias = bias_ref[h] * LOG2E
    uo = uo_ref[...]
    visible = (lax.broadcasted_iota(jnp.int32, (blk, blk), 1)
               < lax.broadcasted_iota(jnp.int32, (blk, blk), 0))

    def keys(kt):
        rows = pl.ds(pl.multiple_of(kt * tile, tile), tile)
        return k_ref[0, rows, :].astype(BF16), v_ref[0, rows, :].astype(BF16)

    k, v = keys(qi)
    zeros = jnp.zeros((blk, blk), F32)
    parts = [_sb_rows(q[r * blk:(r + 1) * blk], k[:(r + 1) * blk], v[:(r + 1) * blk],
                      zeros, zeros, uo, bias, visible) for r in range(nsub)]
    carry = jnp.concatenate([p[0] for p in parts], axis=0)
    acc = jnp.concatenate([p[1] for p in parts], axis=0)

    def body(s, ca):
        k, v = keys(qi - 1 - s)
        return _sb_rows(q, k, v, ca[0], ca[1], uo, bias, None)

    carry, acc = lax.fori_loop(0, qi, body, (carry, acc))
    o_ref[...] = acc.astype(o_ref.dtype)


def _sb_prompt(proj, bias, batch, seq, tile):
    nq = seq // tile
    n = batch * seq
    uo = _strictly_later_const(min(tile, SB_SPAN))
    return pl.pallas_call(
        functools.partial(_sb_prompt_kernel, tile=tile),
        grid=(batch, N_HEADS, nq),
        in_specs=[
            pl.BlockSpec(memory_space=pltpu.SMEM),
            pl.BlockSpec((1, tile, HEAD_DIM), lambda b, h, i: (G_SQ, b * nq + i, h)),
            pl.BlockSpec((1, seq, HEAD_DIM), lambda b, h, i: (G_SK, b, h)),
            pl.BlockSpec((1, seq, HEAD_DIM), lambda b, h, i: (G_SV, b, h)),
            pl.BlockSpec(uo.shape, lambda b, h, i: (0, 0)),
        ],
        out_specs=pl.BlockSpec((tile, HEAD_DIM), lambda b, h, i: (b * nq + i, h)),
        out_shape=jax.ShapeDtypeStruct((n, D_MODEL), BF16),
        compiler_params=pltpu.CompilerParams(
            dimension_semantics=("parallel", "parallel", "arbitrary")),
        name="sb_prompt",
    )(bias, proj, proj, proj, uo)


def _sb_decode_kernel(pt_ref, q_ref, bias_ref, uo_ref, *refs, pages_per_step):
    k_refs = refs[:pages_per_step]
    v_refs = refs[pages_per_step:2 * pages_per_step]
    o_ref, carry_ref, acc_ref = refs[2 * pages_per_step:]
    s = pl.program_id(1)
    scale = HEAD_DIM ** -0.5
    blk = SB_BLOCK
    span = pages_per_step * blk
    head_of_row = lax.broadcasted_iota(jnp.int32, (N_HEADS, span), 0)

    @pl.when(s == 0)
    def _():
        carry_ref[...] = jnp.zeros_like(carry_ref)
        acc_ref[...] = jnp.zeros_like(acc_ref)

    def head_rows(refs, h):
        return jnp.concatenate(
            [ref[0, pl.ds(h, blk, stride=N_HEADS), :] for ref in refs], axis=0).astype(BF16)

    q = q_ref[0, 0].astype(BF16)
    z = jnp.zeros((N_HEADS, span), F32)
    for h in range(N_HEADS):
        z = jnp.where(head_of_row == h, _dot_nt(q, head_rows(k_refs, h)), z)
    z = z * scale + jnp.concatenate([bias_ref[...]] * pages_per_step, axis=1)
    log_b, log_keep = _log_beta_terms(z)
    lk_rows = jnp.concatenate([log_keep[:, r * blk:(r + 1) * blk] for r in range(pages_per_step)], axis=0)
    cs = _exact_dot_rhs01(lk_rows, uo_ref[...])
    carry = carry_ref[...]
    ws = []
    for r in range(pages_per_step):
        rows = slice(r * N_HEADS, (r + 1) * N_HEADS)
        ws.append(jnp.exp(log_b[:, r * blk:(r + 1) * blk] + cs[rows, :blk] + carry).astype(BF16))
        carry = carry + cs[rows, blk:]
    carry_ref[...] = carry
    w = jnp.concatenate(ws, axis=1)
    for h in range(N_HEADS):
        acc_ref[h] += _dot(w, head_rows(v_refs, h))

    @pl.when(s == pl.num_programs(1) - 1)
    def _():
        out = jnp.zeros((N_HEADS, HEAD_DIM), F32)
        for h in range(N_HEADS):
            out = jnp.where(head_of_row[:, :HEAD_DIM] == h, acc_ref[h], out)
        o_ref[0] = out


def _sb_decode(proj, bias_b, cache_k, cache_v, page_table, pages_per_step):
    nb, n_pages = page_table.shape
    n_pool, page = cache_k.shape[0], cache_k.shape[1]
    assert page == SB_BLOCK
    ck = cache_k.reshape(n_pool, page * N_HEADS, HEAD_DIM)
    cv = cache_v.reshape(n_pool, page * N_HEADS, HEAD_DIM)
    proj4 = proj.reshape(N_GROUPS, nb, N_HEADS, HEAD_DIM)
    uo = _later_sum_consts()
    pt = page_table.reshape(-1)

    def page_map(r):
        return lambda b, s, pt: (pt[b * n_pages + n_pages - 1 - (s * pages_per_step + r)], 0, 0)

    def page_specs():
        return [pl.BlockSpec((1, page * N_HEADS, HEAD_DIM), page_map(r)) for r in range(pages_per_step)]

    grid_spec = pltpu.PrefetchScalarGridSpec(
        num_scalar_prefetch=1,
        grid=(nb, n_pages // pages_per_step),
        in_specs=[
            pl.BlockSpec((1, 1, N_HEADS, HEAD_DIM), lambda b, s, pt: (G_SQ, b, 0, 0)),
            pl.BlockSpec(bias_b.shape, lambda b, s, pt: (0, 0)),
            pl.BlockSpec(uo.shape, lambda b, s, pt: (0, 0)),
        ] + page_specs() + page_specs(),
        out_specs=pl.BlockSpec((1, N_HEADS, HEAD_DIM), lambda b, s, pt: (b, 0, 0)),
        scratch_shapes=[pltpu.VMEM((N_HEADS, SB_BLOCK), F32),
                        pltpu.VMEM((N_HEADS, N_HEADS, HEAD_DIM), F32)],
    )
    out = pl.pallas_call(
        functools.partial(_sb_decode_kernel, pages_per_step=pages_per_step),
        grid_spec=grid_spec,
        out_shape=jax.ShapeDtypeStruct((nb, N_HEADS, HEAD_DIM), F32),
        compiler_params=pltpu.CompilerParams(dimension_semantics=("parallel", "arbitrary")),
        name="sb_decode",
    )(pt, proj4, bias_b, uo, *([ck] * pages_per_step), *([cv] * pages_per_step))
    return out.reshape(nb, D_MODEL)


def _merge_kernel(oa_ref, ob_ref, ga_ref, gb_ref, x_ref, wa_ref, wb_ref, wo_ref, gf_ref,
                  x1_ref, h2_ref):
    a = _dot(oa_ref[...].astype(BF16), wa_ref[...])
    b = _dot(ob_ref[...].astype(BF16), wb_ref[...])
    m = jax.nn.sigmoid(ga_ref[0]) * a + jax.nn.sigmoid(gb_ref[0]) * b
    x1 = x_ref[...] + _dot(m.astype(BF16), wo_ref[...])
    x1_ref[...] = x1
    h2_ref[...] = _rms(x1, gf_ref[...]).astype(BF16)


def _merge(o_a, o_b, proj, x, wa, wb, wo, g_ffn, tm):
    n = x.shape[0]
    row = pl.BlockSpec((tm, D_MODEL), lambda i: (i, 0))
    wspec = pl.BlockSpec((D_MODEL, D_MODEL), lambda i: (0, 0))
    return pl.pallas_call(
        _merge_kernel,
        grid=(n // tm,),
        in_specs=[
            row, row,
            pl.BlockSpec((1, tm, D_MODEL), lambda i: (G_GA, i, 0)),
            pl.BlockSpec((1, tm, D_MODEL), lambda i: (G_GB, i, 0)),
            row, wspec, wspec, wspec,
            pl.BlockSpec((1, D_MODEL), lambda i: (0, 0)),
        ],
        out_specs=[row, row],
        out_shape=[jax.ShapeDtypeStruct((n, D_MODEL), F32), jax.ShapeDtypeStruct((n, D_MODEL), BF16)],
        compiler_params=pltpu.CompilerParams(dimension_semantics=("parallel",)),
        name="merge",
    )(o_a, o_b, proj, proj, x, wa, wb, wo, g_ffn)


def _top_rows(vals, order, payload, n):
    out_v, out_p = [], []
    for _ in range(n):
        m = jnp.max(vals, axis=0, keepdims=True)
        first = jnp.min(jnp.where(vals == m, order, 1e9), axis=0, keepdims=True)
        hit = order == first
        out_v.append(m)
        if payload is None:
            out_p.append(first)
        else:
            out_p.append(jnp.max(jnp.where(hit, payload, -1.0), axis=0, keepdims=True))
        vals = jnp.where(hit, NEG_INF, vals)
    return out_v, out_p


def _stack_rows(rows, sub):
    out = jnp.zeros(sub.shape, F32)
    for r, row in enumerate(rows):
        out = jnp.where(sub == r, row, out)
    return out


def _peer_route_kernel(h_ref, wq_ref, sk_ref, i1_ref, i2_ref, gate_ref):
    t = h_ref.shape[0]
    q = _dot(h_ref[...], wq_ref[...]).astype(BF16)
    key_iota = lax.broadcasted_iota(jnp.int32, (PEER_KEYS, t), 0).astype(F32)
    sub16 = lax.broadcasted_iota(jnp.int32, (PEER_TOPK, t), 0)
    sub16f = sub16.astype(F32)
    sub8 = lax.broadcasted_iota(jnp.int32, (8, t), 0)
    sub8f = sub8.astype(F32)
    e_rows, g_rows = [], []
    for h in range(N_HEADS):
        top_v, top_i = [], []
        for p in range(2):
            c0 = (h * 2 + p) * PEER_KEYS
            s = _dot_nt(sk_ref[p], q[:, c0:c0 + PEER_KEYS])
            v, i = _top_rows(s, key_iota, None, PEER_TOPK)
            top_v.append(v)
            top_i.append(i)
        v2 = (_stack_rows(top_v[1][:8], sub8), _stack_rows(top_v[1], sub16))
        i2 = (_stack_rows(top_i[1][:8], sub8), _stack_rows(top_i[1], sub16))
        cv, co, ce = [], [], []
        for a in range(PEER_TOPK):
            nb = PEER_TOPK // (a + 1)
            wide = int(a == 0)
            sub, subf = (sub16, sub16f) if wide else (sub8, sub8f)
            cv.append(jnp.where(sub < nb, top_v[0][a] + v2[wide], NEG_INF))
            co.append(subf + float(a * PEER_TOPK))
            ce.append(top_i[0][a] * float(PEER_KEYS) + i2[wide])
        best, experts = _top_rows(jnp.concatenate(cv, axis=0), jnp.concatenate(co, axis=0),
                                  jnp.concatenate(ce, axis=0), PEER_TOPK)
        top = best[0]
        best = _stack_rows(best, sub16)
        ex = jnp.exp(best - top)
        g_rows.append(ex / jnp.sum(ex, axis=0, keepdims=True))
        e_rows.append(_stack_rows(experts, sub16))
    experts = jnp.concatenate(e_rows, axis=0)
    gates = jnp.concatenate(g_rows, axis=0)
    key1 = jnp.floor(experts * (1.0 / PEER_KEYS))
    i1_ref[...] = key1.T
    i2_ref[...] = (experts - key1 * float(PEER_KEYS)).T
    gate_ref[...] = gates.T


def _peer_route(h2, wq, sub_keys, tt):
    n = h2.shape[0]
    sel = pl.BlockSpec((tt, PEER_SEL), lambda i: (i, 0))
    shp = jax.ShapeDtypeStruct((n, PEER_SEL), F32)
    return pl.pallas_call(
        _peer_route_kernel,
        grid=(n // tt,),
        in_specs=[
            pl.BlockSpec((tt, D_MODEL), lambda i: (i, 0)),
            pl.BlockSpec(wq.shape, lambda i: (0, 0)),
            pl.BlockSpec(sub_keys.shape, lambda i: (0, 0, 0)),
        ],
        out_specs=[sel, sel, sel],
        out_shape=[shp, shp, shp],
        compiler_params=pltpu.CompilerParams(dimension_semantics=("parallel",)),
        name="peer_route",
    )(h2, wq, sub_keys)


def _slab_pitch(tt):
    p = tt // 8
    return 8 * (p + 1 if p % 2 == 0 else p + 2)


def _peer_expert_kernel(h_ref, ut_ref, v_ref, i1_ref, i2_ref, gate_ref, x_ref, gf_ref,
                        y_ref, slab_ref, acc_ref, *, n_chunks, pitch, final_norm):
    s = pl.program_id(1)
    tt = h_ref.shape[0]
    slabs_per_chunk = ut_ref.shape[1] // PEER_KEYS
    key_iota = lax.broadcasted_iota(jnp.int32, (PEER_KEYS, PEER_SEL), 0).astype(F32)

    def slab_rows(g):
        return pl.ds(pl.multiple_of(g * pitch, 8), tt)

    def token_rows(t):
        return pl.ds(t, PEER_KEYS, stride=pitch)

    @pl.when(s < n_chunks)
    def _():
        a = _dot(h_ref[...], ut_ref[...])
        for gl in range(slabs_per_chunk):
            slab_ref[slab_rows(s * slabs_per_chunk + gl), :] = a[:, gl * PEER_KEYS:(gl + 1) * PEER_KEYS]

    @pl.when(s == n_chunks - 1)
    def _():
        sub = lax.broadcasted_iota(jnp.int32, (PEER_GROUP, PEER_SEL), 0)

        def group_base(gi):
            base = gi * PEER_GROUP
            return base if isinstance(base, int) else pl.multiple_of(base, PEER_GROUP)

        def hits(t):
            return (key_iota == i1_ref[pl.ds(t, 1), :],
                    (key_iota == i2_ref[pl.ds(t, 1), :]).astype(BF16))

        def select(gi):
            base = group_base(gi)
            rows = []
            for u in range(PEER_GROUP):
                t = base + u
                hit1, hit2 = hits(t)
                pre = slab_ref[token_rows(t), :]
                col = jnp.broadcast_to(i2_ref[pl.ds(t, 1), :].astype(jnp.int32), (PEER_KEYS, PEER_SEL))
                picked = jnp.take_along_axis(pre, col, axis=1)
                rows.append(jnp.sum(jnp.where(hit1, picked, 0.0), axis=0, keepdims=True))
            return _stack_rows(rows, sub)

        def scatter(gi, pre):
            base = group_base(gi)
            wv = gate_ref[pl.ds(base, PEER_GROUP), :] * (0.5 * pre * (1.0 + lax.erf(pre * (0.5 ** 0.5))))
            for u in range(PEER_GROUP):
                hit1, hit2 = hits(base + u)
                w1 = jnp.where(hit1, wv[u:u + 1, :], 0.0).astype(BF16)
                slab_ref[token_rows(base + u), :] = _dot_nt(w1, hit2)

        def step(gi, pre_prev):
            pre = select(gi)
            scatter(gi - 1, pre_prev)
            return pre

        n_groups = tt // PEER_GROUP
        pre_last = lax.fori_loop(1, n_groups, step, select(0), unroll=PEER_GROUP_UNROLL)
        scatter(n_groups - 1, pre_last)
        acc_ref[...] = jnp.zeros_like(acc_ref)

    @pl.when(s >= n_chunks)
    def _():
        c = s - n_chunks
        w = jnp.concatenate(
            [slab_ref[slab_rows(c * slabs_per_chunk + gl), :].astype(BF16) for gl in range(slabs_per_chunk)],
            axis=1)
        acc_ref[...] += _dot(w, v_ref[...])

    @pl.when(s == 2 * n_chunks - 1)
    def _():
        x2 = x_ref[...] + acc_ref[...]
        y_ref[...] = _rms(x2, gf_ref[...]) if final_norm else x2


def _peer_experts(h2, u_t, v_tab, i1, i2, gates, x1, g_final, final_norm, tt, chunk):
    n = h2.shape[0]
    n_exp = v_tab.shape[0]
    n_chunks = n_exp // chunk
    pitch = _slab_pitch(tt)
    row = pl.BlockSpec((tt, D_MODEL), lambda i, s: (i, 0))
    sel = pl.BlockSpec((tt, PEER_SEL), lambda i, s: (i, 0))
    return pl.pallas_call(
        functools.partial(_peer_expert_kernel, n_chunks=n_chunks, pitch=pitch, final_norm=final_norm),
        grid=(n // tt, 2 * n_chunks),
        in_specs=[
            row,
            pl.BlockSpec((D_MODEL, chunk), lambda i, s: (0, jnp.minimum(s, n_chunks - 1))),
            pl.BlockSpec((chunk, D_MODEL), lambda i, s: (jnp.maximum(s - n_chunks, 0), 0)),
            sel, sel, sel, row,
            pl.BlockSpec((1, D_MODEL), lambda i, s: (0, 0)),
        ],
        out_specs=row,
        out_shape=jax.ShapeDtypeStruct((n, D_MODEL), F32),
        scratch_shapes=[
            pltpu.VMEM((PEER_KEYS * pitch, PEER_KEYS), F32),
            pltpu.VMEM((tt, D_MODEL), F32),
        ],
        compiler_params=pltpu.CompilerParams(dimension_semantics=("parallel", "arbitrary")),
        name="peer_experts",
    )(h2, u_t, v_tab, i1, i2, gates, x1, g_final)


def _tile(n, target):
    return min(n, target)


def kernel(x_prompt, x_sample, cache_k, cache_v, state_hgrn, page_table, g_attn, w_in, hg_lb_logits,
           hg_norm, sb_bias, w_branch_a, w_branch_b, w_out, g_ffn, peer_wq, peer_subkeys, peer_u, peer_v,
           g_final):
    depth = w_in.shape[0]
    batch, seq, _ = x_prompt.shape
    nb, dec_seq, _ = x_sample.shape
    assert dec_seq == 1
    n_p = batch * seq
    xp = x_prompt.reshape(n_p, D_MODEL)
    xs = x_sample.reshape(nb, D_MODEL)
    g_fin = g_final.reshape(1, D_MODEL)
    outs = {k: [] for k in ("sp", "kp", "vp", "ss", "ks", "vs")}
    for l in range(depth):
        w_in_l = w_in[l].astype(BF16)
        wa, wb, wo = (w[l].astype(BF16) for w in (w_branch_a, w_branch_b, w_out))
        wq = peer_wq[l].astype(BF16)
        sk = peer_subkeys[l].astype(BF16)
        u_t = peer_u[l].astype(BF16).T
        v_tab = peer_v[l].astype(BF16)
        g_a = g_attn[l].reshape(1, D_MODEL)
        g_f = g_ffn[l].reshape(1, D_MODEL)
        ng = hg_norm[l].reshape(1, D_MODEL)
        bias_b = jnp.broadcast_to(sb_bias[l].astype(F32)[:, None], (N_HEADS, SB_BLOCK))
        last = l == depth - 1

        proj, k_new, v_new = _norm_proj(xp, g_a, w_in_l, _tile(n_p, 1024))
        o_a, s_p = _hgrn_prompt(proj, hg_lb_logits, ng, batch, seq, l, _tile(seq, 512))
        o_b = _sb_prompt(proj, sb_bias[l].astype(F32), batch, seq, _tile(seq, SB_TILE))
        x1, h2 = _merge(o_a, o_b, proj, xp, wa, wb, wo, g_f, _tile(n_p, 512))
        i1, i2, gates = _peer_route(h2, wq, sk, _tile(n_p, 256))
        xp = _peer_experts(h2, u_t, v_tab, i1, i2, gates, x1, g_fin, last, _tile(n_p, 512), 1024)
        outs["sp"].append(s_p)
        outs["kp"].append(k_new.reshape(batch, seq, N_HEADS, HEAD_DIM))
        outs["vp"].append(v_new.reshape(batch, seq, N_HEADS, HEAD_DIM))

        proj_s, k_new_s, v_new_s = _norm_proj(xs, g_a, w_in_l, nb)
        o_a_s, s_s = _hgrn_step(proj_s, hg_lb_logits, hg_norm[l], state_hgrn[l], l)
        o_b_s = _sb_decode(proj_s, bias_b, cache_k[l], cache_v[l], page_table, 16)
        x1_s, h2_s = _merge(o_a_s.reshape(nb, D_MODEL), o_b_s, proj_s, xs, wa, wb, wo, g_f, nb)
        pad = (-nb) % PEER_KEYS
        sel_s = _peer_route(jnp.pad(h2_s, ((0, pad), (0, 0))), wq, sk, PEER_KEYS)
        i1_s, i2_s, gates_s = (a[:nb] for a in sel_s)
        xs = _peer_experts(h2_s, u_t, v_tab, i1_s, i2_s, gates_s, x1_s, g_fin, last, nb, 2048)
        outs["ss"].append(s_s)
        outs["ks"].append(k_new_s.reshape(nb, 1, N_HEADS, HEAD_DIM))
        outs["vs"].append(v_new_s.reshape(nb, 1, N_HEADS, HEAD_DIM))

    y_prompt = xp.reshape(batch, seq, D_MODEL)
    y_sample = xs.reshape(nb, 1, D_MODEL)
    return (y_prompt, y_sample, jnp.stack(outs["sp"]), jnp.stack(outs["kp"]), jnp.stack(outs["vp"]),
            jnp.stack(outs["ss"]), jnp.stack(outs["ks"]), jnp.stack(outs["vs"]))
```

```python
import functools

import numpy as np
import jax
import jax.numpy as jnp
from jax import lax
from jax.experimental import pallas as pl
from jax.experimental.pallas import tpu as pltpu

F32 = jnp.float32
BF16 = jnp.bfloat16

D_MODEL = 1024
N_HEADS = 8
HEAD_DIM = 128
N_GROUPS = 9
HG_CHUNK = 128
SB_BLOCK = 128
SB_TILE = 1024
SB_SPAN = 256
PEER_GROUP = 8
PEER_GROUP_UNROLL = 2
PEER_KEYS = 128
PEER_TOPK = 16
PEER_SEL = N_HEADS * PEER_TOPK
RMS_EPS = 1e-6
NEG_INF = float("-inf")
LOG2E = 1.4426950408889634

G_HQ, G_HF, G_HI, G_HG, G_SQ, G_SK, G_SV, G_GA, G_GB = range(N_GROUPS)


def _dot(a, b):
    return jnp.dot(a, b, preferred_element_type=F32)


def _dot_nt(a, b):
    return lax.dot_general(a, b, (((1,), (1,)), ((), ())), preferred_element_type=F32)


def _dot_tn(a, b):
    return lax.dot_general(a, b, (((0,), (0,)), ((), ())), preferred_element_type=F32)


def _split2(x):
    hi = x.astype(BF16)
    lo = (x - hi.astype(F32)).astype(BF16)
    return hi, lo


def _split3(x):
    hi = x.astype(BF16)
    r = x - hi.astype(F32)
    mid = r.astype(BF16)
    lo = (r - mid.astype(F32)).astype(BF16)
    return hi, mid, lo


def _exact_dot_lhs01(m01, x):
    n = x.shape[1]
    parts = _dot(m01, jnp.concatenate(_split3(x), axis=1))
    return parts[:, :n] + parts[:, n:2 * n] + parts[:, 2 * n:]


def _exact_dot_rhs01(x, m01):
    hi, mid, lo = _split3(x)
    return _dot(hi, m01) + _dot(mid, m01) + _dot(lo, m01)


def _rms(x, g):
    return x * lax.rsqrt(jnp.mean(x * x, axis=-1, keepdims=True) + RMS_EPS) * g


def _silu(x):
    return x * jax.nn.sigmoid(x)


def _norm_proj_kernel(x_ref, g_ref, w_ref, o_ref, k_ref, v_ref, h_ref):
    j = pl.program_id(1)
    tm = x_ref.shape[0]

    @pl.when(j == 0)
    def _():
        h_ref[...] = _rms(x_ref[...], g_ref[...]).astype(BF16)

    out = _dot(h_ref[...], w_ref[...])
    o_ref[0] = out

    for group, ref in ((G_SK, k_ref), (G_SV, v_ref)):
        @pl.when(j == group)
        def _():
            for h in range(N_HEADS):
                ref[pl.ds(h, tm, stride=N_HEADS), :] = out[:, h * HEAD_DIM:(h + 1) * HEAD_DIM]


def _norm_proj(x, g, w, tm):
    n = x.shape[0]
    kv_spec = pl.BlockSpec((tm * N_HEADS, HEAD_DIM), lambda i, j: (i, 0))
    kv_shape = jax.ShapeDtypeStruct((n * N_HEADS, HEAD_DIM), F32)
    return pl.pallas_call(
        _norm_proj_kernel,
        grid=(n // tm, N_GROUPS),
        in_specs=[
            pl.BlockSpec((tm, D_MODEL), lambda i, j: (i, 0)),
            pl.BlockSpec((1, D_MODEL), lambda i, j: (0, 0)),
            pl.BlockSpec((D_MODEL, D_MODEL), lambda i, j: (0, j)),
        ],
        out_specs=[pl.BlockSpec((1, tm, D_MODEL), lambda i, j: (j, i, 0)), kv_spec, kv_spec],
        out_shape=[jax.ShapeDtypeStruct((N_GROUPS, n, D_MODEL), F32), kv_shape, kv_shape],
        scratch_shapes=[pltpu.VMEM((tm, D_MODEL), BF16)],
        compiler_params=pltpu.CompilerParams(dimension_semantics=("parallel", "arbitrary")),
        name="norm_proj",
    )(x, g, w)


def _hgrn_consts():
    c = HG_CHUNK
    t = np.arange(c)[:, None]
    j = np.arange(c)[None, :]
    sel = [(j <= t)]
    masks = []
    m = c // 2
    while m >= 1:
        mid = (t // (2 * m)) * (2 * m) + m
        upper = t >= mid
        d = np.where(upper, (j >= mid) & (j <= t), (j > t) & (j <= mid - 1))
        sel.append(d)
        s = j
        same = (t // (2 * m)) == (s // (2 * m))
        s_mid = (s // (2 * m)) * (2 * m) + m
        masks.append(same & upper & (s < s_mid))
        m //= 2
    sel.append(j > t)
    masks.append(t == j)
    sel = np.concatenate(sel, axis=0).astype(np.float32)
    masks = np.stack(masks).astype(np.float32)
    return jnp.asarray(sel, dtype=BF16), jnp.asarray(masks, dtype=F32)


def _lower_bound(lbl, layer):
    e = jnp.exp(lbl - jnp.max(lbl, axis=0, keepdims=True))
    return jnp.sum(e[: layer + 1], axis=0, keepdims=True) / jnp.sum(e, axis=0, keepdims=True)


def _hgrn_kernel(q_ref, f_ref, i_ref, g_ref, lbl_ref, ng_ref, sel_ref, mask_ref,
                 o_ref, s_ref, st_ref, *, n_sub, layer):
    c = pl.program_id(2)
    n_lvl = mask_ref.shape[0] - 1
    ch = HG_CHUNK

    @pl.when(c == 0)
    def _():
        st_ref[...] = jnp.zeros_like(st_ref)

    lb = _lower_bound(lbl_ref[...], layer)
    ng = ng_ref[...]
    sel = sel_ref[...]

    chunks = []
    for sc in range(n_sub):
        rows = slice(sc * ch, (sc + 1) * ch)
        f = lb + (1.0 - lb) * jax.nn.sigmoid(f_ref[0, rows, :])
        logf = jnp.log(f)
        k = 1.0 - f
        q = _silu(q_ref[0, rows, :])
        v = i_ref[0, rows, :].astype(BF16)
        ex = jnp.exp(_exact_dot_lhs01(sel, logf))
        scores = _dot_nt(q.astype(BF16), k.astype(BF16)) * mask_ref[n_lvl]
        for li in range(n_lvl):
            el = ex[(li + 1) * ch:(li + 2) * ch]
            scores += _dot_nt((q * el).astype(BF16), (k * el).astype(BF16)) * mask_ref[li]
        e_last = ex[(n_lvl + 1) * ch:(n_lvl + 2) * ch]
        chunks.append((
            (q * ex[0:ch]).astype(BF16),
            _dot(scores.astype(BF16), v),
            ex[ch - 1:ch],
            _dot_tn(v, (k * e_last).astype(BF16)),
        ))

    st = st_ref[...]
    for sc, (q_dec, o_intra, decay, kv) in enumerate(chunks):
        rows = slice(sc * ch, (sc + 1) * ch)
        o = o_intra + _dot_nt(q_dec, st.astype(BF16))
        st = st * decay + kv
        o = o * lax.rsqrt(jnp.mean(o * o, axis=-1, keepdims=True) + RMS_EPS)
        o_ref[rows, :] = (o * ng * _silu(g_ref[0, rows, :])).astype(o_ref.dtype)
    st_ref[...] = st

    @pl.when(c == pl.num_programs(2) - 1)
    def _():
        s_ref[0, 0] = st_ref[...].T


def _hgrn_prompt(proj, lb_logits, norm_g, batch, seq, layer, tc):
    sel, masks = _hgrn_consts()
    nc = seq // tc
    n = batch * seq

    def pmap(g):
        return lambda b, h, c: (g, b * nc + c, h)

    kern = functools.partial(_hgrn_kernel, n_sub=tc // HG_CHUNK, layer=layer)
    return pl.pallas_call(
        kern,
        grid=(batch, N_HEADS, nc),
        in_specs=[
            pl.BlockSpec((1, tc, HEAD_DIM), pmap(G_HQ)),
            pl.BlockSpec((1, tc, HEAD_DIM), pmap(G_HF)),
            pl.BlockSpec((1, tc, HEAD_DIM), pmap(G_HI)),
            pl.BlockSpec((1, tc, HEAD_DIM), pmap(G_HG)),
            pl.BlockSpec((lb_logits.shape[0], HEAD_DIM), lambda b, h, c: (0, h)),
            pl.BlockSpec((1, HEAD_DIM), lambda b, h, c: (0, h)),
            pl.BlockSpec(sel.shape, lambda b, h, c: (0, 0)),
            pl.BlockSpec(masks.shape, lambda b, h, c: (0, 0, 0)),
        ],
        out_specs=[
            pl.BlockSpec((tc, HEAD_DIM), lambda b, h, c: (b * nc + c, h)),
            pl.BlockSpec((1, 1, HEAD_DIM, HEAD_DIM), lambda b, h, c: (b, h, 0, 0)),
        ],
        out_shape=[
            jax.ShapeDtypeStruct((n, D_MODEL), BF16),
            jax.ShapeDtypeStruct((batch, N_HEADS, HEAD_DIM, HEAD_DIM), F32),
        ],
        scratch_shapes=[pltpu.VMEM((HEAD_DIM, HEAD_DIM), F32)],
        compiler_params=pltpu.CompilerParams(
            dimension_semantics=("parallel", "parallel", "arbitrary")),
        name="hgrn_prompt",
    )(proj, proj, proj, proj, lb_logits, norm_g, sel, masks)


def _hgrn_step_kernel(q_ref, f_ref, i_ref, g_ref, lbl_ref, ng_ref, s_ref, o_ref, sn_ref, *, layer):
    lb = _lower_bound(lbl_ref[...], layer)
    lb = lb[0]
    f = lb + (1.0 - lb) * jax.nn.sigmoid(f_ref[0, 0])
    q = _silu(q_ref[0, 0])
    v = i_ref[0, 0]
    g = g_ref[0, 0]
    ng = ng_ref[...]
    pad = jnp.zeros((HEAD_DIM - 2 * N_HEADS, HEAD_DIM), F32)
    cols = jnp.concatenate([f, q, pad], axis=0).T
    for h in range(N_HEADS):
        fcol = cols[:, h:h + 1]
        qcol = cols[:, N_HEADS + h:N_HEADS + h + 1]
        s_new = fcol * s_ref[0, h] + (1.0 - fcol) * v[h:h + 1, :]
        sn_ref[0, h] = s_new
        o = jnp.sum(qcol * s_new, axis=0, keepdims=True)
        o = o * lax.rsqrt(jnp.mean(o * o, axis=-1, keepdims=True) + RMS_EPS)
        o_ref[0, pl.ds(h, 1), :] = o * ng[h:h + 1, :] * _silu(g[h:h + 1, :])


def _hgrn_step(proj, lb_logits, norm_g, state, layer):
    nb = state.shape[0]
    proj4 = proj.reshape(N_GROUPS, nb, N_HEADS, HEAD_DIM)
    lbl = lb_logits.reshape(lb_logits.shape[0], N_HEADS, HEAD_DIM)
    ng = norm_g.reshape(N_HEADS, HEAD_DIM)

    def pmap(g):
        return lambda b: (g, b, 0, 0)

    blk = (1, 1, N_HEADS, HEAD_DIM)
    return pl.pallas_call(
        functools.partial(_hgrn_step_kernel, layer=layer),
        grid=(nb,),
        in_specs=[
            pl.BlockSpec(blk, pmap(G_HQ)),
            pl.BlockSpec(blk, pmap(G_HF)),
            pl.BlockSpec(blk, pmap(G_HI)),
            pl.BlockSpec(blk, pmap(G_HG)),
            pl.BlockSpec(lbl.shape, lambda b: (0, 0, 0)),
            pl.BlockSpec(ng.shape, lambda b: (0, 0)),
            pl.BlockSpec((1, N_HEADS, HEAD_DIM, HEAD_DIM), lambda b: (b, 0, 0, 0)),
        ],
        out_specs=[
            pl.BlockSpec((1, N_HEADS, HEAD_DIM), lambda b: (b, 0, 0)),
            pl.BlockSpec((1, N_HEADS, HEAD_DIM, HEAD_DIM), lambda b: (b, 0, 0, 0)),
        ],
        out_shape=[
            jax.ShapeDtypeStruct((nb, N_HEADS, HEAD_DIM), F32),
            jax.ShapeDtypeStruct(state.shape, F32),
        ],
        name="hgrn_step",
    )(proj4, proj4, proj4, proj4, lbl, ng, state)


def _later_sum_consts():
    j = np.arange(SB_BLOCK)[:, None]
    k = np.arange(SB_BLOCK)[None, :]
    u = np.concatenate([(j > k), np.ones((SB_BLOCK, SB_BLOCK), bool)], axis=1)
    return jnp.asarray(u.astype(np.float32), dtype=BF16)


def _log_beta_terms(z):
    l1p = jnp.log(1.0 + jnp.exp(-jnp.abs(z)))
    return jnp.minimum(z, 0.0) - l1p, -jnp.maximum(z, 0.0) - l1p


def _strictly_later_const(span):
    j = np.arange(span)[:, None]
    k = np.arange(span)[None, :]
    return jnp.asarray((j > k).astype(np.float32), dtype=BF16)


def _mask_last_block(x, visible):
    cut = x.shape[1] - SB_BLOCK
    last = jnp.where(visible, x[:, cut:], 0.0)
    return last if cut == 0 else jnp.concatenate([x[:, :cut], last], axis=1)


def _sb_rows(q, k, v, carry, acc, later01, bias, visible):
    blk = SB_BLOCK
    span = later01.shape[0]
    n_keys = k.shape[0]
    z = _dot_nt(q, k) + bias
    neg_abs = lax.bitcast_convert_type(lax.bitcast_convert_type(z, jnp.uint32) | jnp.uint32(0x80000000), F32)
    log_b = jnp.minimum(z, 0.0) - jnp.log2(1.0 + jnp.exp2(neg_abs))
    log_keep = log_b - z
    if visible is not None:
        log_keep = _mask_last_block(log_keep, visible)
    ws = []
    for c0 in reversed(range(0, n_keys, span)):
        wd = min(span, n_keys - c0)
        lk = log_keep[:, c0:c0 + wd]
        u = later01 if wd == span else later01[:wd, :wd]
        later = _dot(lk.astype(BF16), u)
        ws.insert(0, jnp.exp2(log_b[:, c0:c0 + wd] + later + jnp.concatenate([carry] * (wd // blk), axis=1)))
        carry = carry + jnp.sum(lk, axis=1, keepdims=True)
    w = jnp.concatenate(ws, axis=1)
    if visible is not None:
        w = _mask_last_block(w, visible)
    return carry, acc + _dot(w.astype(BF16), v)


def _sb_prompt_kernel(bias_ref, q_ref, k_ref, v_ref, uo_ref, o_ref, *, tile):
    h = pl.program_id(1)
    qi = pl.program_id(2)
    blk = SB_BLOCK
    nsub = tile // blk
    q = (q_ref[0] * (LOG2E * HEAD_DIM ** -0.5)).astype(BF16)
    bias = bias_ref[h] * LOG2E
    uo = uo_ref[...]
    visible = (lax.broadcasted_iota(jnp.int32, (blk, blk), 1)
               < lax.broadcasted_iota(jnp.int32, (blk, blk), 0))

    def keys(kt):
        rows = pl.ds(pl.multiple_of(kt * tile, tile), tile)
        return k_ref[0, rows, :].astype(BF16), v_ref[0, rows, :].astype(BF16)

    k, v = keys(qi)
    zeros = jnp.zeros((blk, blk), F32)
    parts = [_sb_rows(q[r * blk:(r + 1) * blk], k[:(r + 1) * blk], v[:(r + 1) * blk],
                      zeros, zeros, uo, bias, visible) for r in range(nsub)]
    carry = jnp.concatenate([p[0] for p in parts], axis=0)
    acc = jnp.concatenate([p[1] for p in parts], axis=0)

    def body(s, ca):
        k, v = keys(qi - 1 - s)
        return _sb_rows(q, k, v, ca[0], ca[1], uo, bias, None)

    carry, acc = lax.fori_loop(0, qi, body, (carry, acc))
    o_ref[...] = acc.astype(o_ref.dtype)


def _sb_prompt(proj, bias, batch, seq, tile):
    nq = seq // tile
    n = batch * seq
    uo = _strictly_later_const(min(tile, SB_SPAN))
    return pl.pallas_call(
        functools.partial(_sb_prompt_kernel, tile=tile),
        grid=(batch, N_HEADS, nq),
        in_specs=[
            pl.BlockSpec(memory_space=pltpu.SMEM),
            pl.BlockSpec((1, tile, HEAD_DIM), lambda b, h, i: (G_SQ, b * nq + i, h)),
            pl.BlockSpec((1, seq, HEAD_DIM), lambda b, h, i: (G_SK, b, h)),
            pl.BlockSpec((1, seq, HEAD_DIM), lambda b, h, i: (G_SV, b, h)),
            pl.BlockSpec(uo.shape, lambda b, h, i: (0, 0)),
        ],
        out_specs=pl.BlockSpec((tile, HEAD_DIM), lambda b, h, i: (b * nq + i, h)),
        out_shape=jax.ShapeDtypeStruct((n, D_MODEL), BF16),
        compiler_params=pltpu.CompilerParams(
            dimension_semantics=("parallel", "parallel", "arbitrary")),
        name="sb_prompt",
    )(bias, proj, proj, proj, uo)


def _sb_decode_kernel(pt_ref, q_ref, bias_ref, uo_ref, *refs, pages_per_step):
    k_refs = refs[:pages_per_step]
    v_refs = refs[pages_per_step:2 * pages_per_step]
    o_ref, carry_ref, acc_ref = refs[2 * pages_per_step:]
    s = pl.program_id(1)
    scale = HEAD_DIM ** -0.5
    blk = SB_BLOCK
    span = pages_per_step * blk
    head_of_row = lax.broadcasted_iota(jnp.int32, (N_HEADS, span), 0)

    @pl.when(s == 0)
    def _():
        carry_ref[...] = jnp.zeros_like(carry_ref)
        acc_ref[...] = jnp.zeros_like(acc_ref)

    def head_rows(refs, h):
        return jnp.concatenate(
            [ref[0, pl.ds(h, blk, stride=N_HEADS), :] for ref in refs], axis=0).astype(BF16)

    q = q_ref[0, 0].astype(BF16)
    z = jnp.zeros((N_HEADS, span), F32)
    for h in range(N_HEADS):
        z = jnp.where(head_of_row == h, _dot_nt(q, head_rows(k_refs, h)), z)
    z = z * scale + jnp.concatenate([bias_ref[...]] * pages_per_step, axis=1)
    log_b, log_keep = _log_beta_terms(z)
    lk_rows = jnp.concatenate([log_keep[:, r * blk:(r + 1) * blk] for r in range(pages_per_step)], axis=0)
    cs = _exact_dot_rhs01(lk_rows, uo_ref[...])
    carry = carry_ref[...]
    ws = []
    for r in range(pages_per_step):
        rows = slice(r * N_HEADS, (r + 1) * N_HEADS)
        ws.append(jnp.exp(log_b[:, r * blk:(r + 1) * blk] + cs[rows, :blk] + carry).astype(BF16))
        carry = carry + cs[rows, blk:]
    carry_ref[...] = carry
    w = jnp.concatenate(ws, axis=1)
    for h in range(N_HEADS):
        acc_ref[h] += _dot(w, head_rows(v_refs, h))

    @pl.when(s == pl.num_programs(1) - 1)
    def _():
        out = jnp.zeros((N_HEADS, HEAD_DIM), F32)
        for h in range(N_HEADS):
            out = jnp.where(head_of_row[:, :HEAD_DIM] == h, acc_ref[h], out)
        o_ref[0] = out


def _sb_decode(proj, bias_b, cache_k, cache_v, page_table, pages_per_step):
    nb, n_pages = page_table.shape
    n_pool, page = cache_k.shape[0], cache_k.shape[1]
    assert page == SB_BLOCK
    ck = cache_k.reshape(n_pool, page * N_HEADS, HEAD_DIM)
    cv = cache_v.reshape(n_pool, page * N_HEADS, HEAD_DIM)
    proj4 = proj.reshape(N_GROUPS, nb, N_HEADS, HEAD_DIM)
    uo = _later_sum_consts()
    pt = page_table.reshape(-1)

    def page_map(r):
        return lambda b, s, pt: (pt[b * n_pages + n_pages - 1 - (s * pages_per_step + r)], 0, 0)

    def page_specs():
        return [pl.BlockSpec((1, page * N_HEADS, HEAD_DIM), page_map(r)) for r in range(pages_per_step)]

    grid_spec = pltpu.PrefetchScalarGridSpec(
        num_scalar_prefetch=1,
        grid=(nb, n_pages // pages_per_step),
        in_specs=[
            pl.BlockSpec((1, 1, N_HEADS, HEAD_DIM), lambda b, s, pt: (G_SQ, b, 0, 0)),
            pl.BlockSpec(bias_b.shape, lambda b, s, pt: (0, 0)),
            pl.BlockSpec(uo.shape, lambda b, s, pt: (0, 0)),
        ] + page_specs() + page_specs(),
        out_specs=pl.BlockSpec((1, N_HEADS, HEAD_DIM), lambda b, s, pt: (b, 0, 0)),
        scratch_shapes=[pltpu.VMEM((N_HEADS, SB_BLOCK), F32),
                        pltpu.VMEM((N_HEADS, N_HEADS, HEAD_DIM), F32)],
    )
    out = pl.pallas_call(
        functools.partial(_sb_decode_kernel, pages_per_step=pages_per_step),
        grid_spec=grid_spec,
        out_shape=jax.ShapeDtypeStruct((nb, N_HEADS, HEAD_DIM), F32),
        compiler_params=pltpu.CompilerParams(dimension_semantics=("parallel", "arbitrary")),
        name="sb_decode",
    )(pt, proj4, bias_b, uo, *([ck] * pages_per_step), *([cv] * pages_per_step))
    return out.reshape(nb, D_MODEL)


def _merge_kernel(oa_ref, ob_ref, ga_ref, gb_ref, x_ref, wa_ref, wb_ref, wo_ref, gf_ref,
                  x1_ref, h2_ref):
    a = _dot(oa_ref[...].astype(BF16), wa_ref[...])
    b = _dot(ob_ref[...].astype(BF16), wb_ref[...])
    m = jax.nn.sigmoid(ga_ref[0]) * a + jax.nn.sigmoid(gb_ref[0]) * b
    x1 = x_ref[...] + _dot(m.astype(BF16), wo_ref[...])
    x1_ref[...] = x1
    h2_ref[...] = _rms(x1, gf_ref[...]).astype(BF16)


def _merge(o_a, o_b, proj, x, wa, wb, wo, g_ffn, tm):
    n = x.shape[0]
    row = pl.BlockSpec((tm, D_MODEL), lambda i: (i, 0))
    wspec = pl.BlockSpec((D_MODEL, D_MODEL), lambda i: (0, 0))
    return pl.pallas_call(
        _merge_kernel,
        grid=(n // tm,),
        in_specs=[
            row, row,
            pl.BlockSpec((1, tm, D_MODEL), lambda i: (G_GA, i, 0)),
            pl.BlockSpec((1, tm, D_MODEL), lambda i: (G_GB, i, 0)),
            row, wspec, wspec, wspec,
            pl.BlockSpec((1, D_MODEL), lambda i: (0, 0)),
        ],
        out_specs=[row, row],
        out_shape=[jax.ShapeDtypeStruct((n, D_MODEL), F32), jax.ShapeDtypeStruct((n, D_MODEL), BF16)],
        compiler_params=pltpu.CompilerParams(dimension_semantics=("parallel",)),
        name="merge",
    )(o_a, o_b, proj, proj, x, wa, wb, wo, g_ffn)


def _top_rows(vals, order, payload, n):
    out_v, out_p = [], []
    for _ in range(n):
        m = jnp.max(vals, axis=0, keepdims=True)
        first = jnp.min(jnp.where(vals == m, order, 1e9), axis=0, keepdims=True)
        hit = order == first
        out_v.append(m)
        if payload is None:
            out_p.append(first)
        else:
            out_p.append(jnp.max(jnp.where(hit, payload, -1.0), axis=0, keepdims=True))
        vals = jnp.where(hit, NEG_INF, vals)
    return out_v, out_p


def _stack_rows(rows, sub):
    out = jnp.zeros(sub.shape, F32)
    for r, row in enumerate(rows):
        out = jnp.where(sub == r, row, out)
    return out


def _peer_route_kernel(h_ref, wq_ref, sk_ref, i1_ref, i2_ref, gate_ref):
    t = h_ref.shape[0]
    q = _dot(h_ref[...], wq_ref[...]).astype(BF16)
    key_iota = lax.broadcasted_iota(jnp.int32, (PEER_KEYS, t), 0).astype(F32)
    sub16 = lax.broadcasted_iota(jnp.int32, (PEER_TOPK, t), 0)
    sub16f = sub16.astype(F32)
    sub8 = lax.broadcasted_iota(jnp.int32, (8, t), 0)
    sub8f = sub8.astype(F32)
    e_rows, g_rows = [], []
    for h in range(N_HEADS):
        top_v, top_i = [], []
        for p in range(2):
            c0 = (h * 2 + p) * PEER_KEYS
            s = _dot_nt(sk_ref[p], q[:, c0:c0 + PEER_KEYS])
            v, i = _top_rows(s, key_iota, None, PEER_TOPK)
            top_v.append(v)
            top_i.append(i)
        v2 = (_stack_rows(top_v[1][:8], sub8), _stack_rows(top_v[1], sub16))
        i2 = (_stack_rows(top_i[1][:8], sub8), _stack_rows(top_i[1], sub16))
        cv, co, ce = [], [], []
        for a in range(PEER_TOPK):
            nb = PEER_TOPK // (a + 1)
            wide = int(a == 0)
            sub, subf = (sub16, sub16f) if wide else (sub8, sub8f)
            cv.append(jnp.where(sub < nb, top_v[0][a] + v2[wide], NEG_INF))
            co.append(subf + float(a * PEER_TOPK))
            ce.append(top_i[0][a] * float(PEER_KEYS) + i2[wide])
        best, experts = _top_rows(jnp.concatenate(cv, axis=0), jnp.concatenate(co, axis=0),
                                  jnp.concatenate(ce, axis=0), PEER_TOPK)
        top = best[0]
        best = _stack_rows(best, sub16)
        ex = jnp.exp(best - top)
        g_rows.append(ex / jnp.sum(ex, axis=0, keepdims=True))
        e_rows.append(_stack_rows(experts, sub16))
    experts = jnp.concatenate(e_rows, axis=0)
    gates = jnp.concatenate(g_rows, axis=0)
    key1 = jnp.floor(experts * (1.0 / PEER_KEYS))
    i1_ref[...] = key1.T
    i2_ref[...] = (experts - key1 * float(PEER_KEYS)).T
    gate_ref[...] = gates.T


def _peer_route(h2, wq, sub_keys, tt):
    n = h2.shape[0]
    sel = pl.BlockSpec((tt, PEER_SEL), lambda i: (i, 0))
    shp = jax.ShapeDtypeStruct((n, PEER_SEL), F32)
    return pl.pallas_call(
        _peer_route_kernel,
        grid=(n // tt,),
        in_specs=[
            pl.BlockSpec((tt, D_MODEL), lambda i: (i, 0)),
            pl.BlockSpec(wq.shape, lambda i: (0, 0)),
            pl.BlockSpec(sub_keys.shape, lambda i: (0, 0, 0)),
        ],
        out_specs=[sel, sel, sel],
        out_shape=[shp, shp, shp],
        compiler_params=pltpu.CompilerParams(dimension_semantics=("parallel",)),
        name="peer_route",
    )(h2, wq, sub_keys)


def _slab_pitch(tt):
    p = tt // 8
    return 8 * (p + 1 if p % 2 == 0 else p + 2)


def _peer_expert_kernel(h_ref, ut_ref, v_ref, i1_ref, i2_ref, gate_ref, x_ref, gf_ref,
                        y_ref, slab_ref, *, n_up, n_down, pitch, final_norm):
    s = pl.program_id(1)
    tt = h_ref.shape[0]
    slabs_up = ut_ref.shape[1] // PEER_KEYS
    slabs_down = v_ref.shape[0] // PEER_KEYS
    key_iota = lax.broadcasted_iota(jnp.int32, (PEER_KEYS, PEER_SEL), 0).astype(F32)

    def slab_rows(g):
        return pl.ds(pl.multiple_of(g * pitch, 8), tt)

    def token_rows(t):
        return pl.ds(t, PEER_KEYS, stride=pitch)

    @pl.when(s < n_up)
    def _():
        a = _dot(h_ref[...], ut_ref[...])
        for gl in range(slabs_up):
            slab_ref[slab_rows(s * slabs_up + gl), :] = a[:, gl * PEER_KEYS:(gl + 1) * PEER_KEYS]

    @pl.when(s == n_up - 1)
    def _():
        sub = lax.broadcasted_iota(jnp.int32, (PEER_GROUP, PEER_SEL), 0)

        def group_base(gi):
            base = gi * PEER_GROUP
            return base if isinstance(base, int) else pl.multiple_of(base, PEER_GROUP)

        def hits(t):
            return (key_iota == i1_ref[pl.ds(t, 1), :],
                    (key_iota == i2_ref[pl.ds(t, 1), :]).astype(BF16))

        def select(gi):
            base = group_base(gi)
            rows = []
            for u in range(PEER_GROUP):
                t = base + u
                hit1, hit2 = hits(t)
                pre = slab_ref[token_rows(t), :]
                col = jnp.broadcast_to(i2_ref[pl.ds(t, 1), :].astype(jnp.int32), (PEER_KEYS, PEER_SEL))
                picked = jnp.take_along_axis(pre, col, axis=1)
                rows.append(jnp.sum(jnp.where(hit1, picked, 0.0), axis=0, keepdims=True))
            return _stack_rows(rows, sub)

        def scatter(gi, pre):
            base = group_base(gi)
            wv = gate_ref[pl.ds(base, PEER_GROUP), :] * (0.5 * pre * (1.0 + lax.erf(pre * (0.5 ** 0.5))))
            for u in range(PEER_GROUP):
                hit1, hit2 = hits(base + u)
                w1 = jnp.where(hit1, wv[u:u + 1, :], 0.0).astype(BF16)
                slab_ref[token_rows(base + u), :] = _dot_nt(w1, hit2)

        def step(gi, pre_prev):
            pre = select(gi)
            scatter(gi - 1, pre_prev)
            return pre

        n_groups = tt // PEER_GROUP
        pre_last = lax.fori_loop(1, n_groups, step, select(0), unroll=PEER_GROUP_UNROLL)
        scatter(n_groups - 1, pre_last)
        y_ref[...] = jnp.zeros_like(y_ref)

    @pl.when(s >= n_up)
    def _():
        c = s - n_up
        w = jnp.concatenate(
            [slab_ref[slab_rows(c * slabs_down + gl), :].astype(BF16) for gl in range(slabs_down)], axis=1)
        y_ref[...] += _dot(w, v_ref[...])

    @pl.when(s == n_up + n_down - 1)
    def _():
        x2 = x_ref[...] + y_ref[...]
        y_ref[...] = _rms(x2, gf_ref[...]) if final_norm else x2


def _peer_experts(h2, u_t, v_tab, i1, i2, gates, x1, g_final, final_norm, tt, chunk_up, chunk_down):
    n = h2.shape[0]
    n_exp = v_tab.shape[0]
    n_up, n_down = n_exp // chunk_up, n_exp // chunk_down
    pitch = _slab_pitch(tt)
    row = pl.BlockSpec((tt, D_MODEL), lambda i, s: (i, 0))
    sel = pl.BlockSpec((tt, PEER_SEL), lambda i, s: (i, 0))
    return pl.pallas_call(
        functools.partial(_peer_expert_kernel, n_up=n_up, n_down=n_down, pitch=pitch, final_norm=final_norm),
        grid=(n // tt, n_up + n_down),
        in_specs=[
            row,
            pl.BlockSpec((D_MODEL, chunk_up), lambda i, s: (0, jnp.minimum(s, n_up - 1))),
            pl.BlockSpec((chunk_down, D_MODEL), lambda i, s: (jnp.maximum(s - n_up, 0), 0)),
            sel, sel, sel, row,
            pl.BlockSpec((1, D_MODEL), lambda i, s: (0, 0)),
        ],
        out_specs=row,
        out_shape=jax.ShapeDtypeStruct((n, D_MODEL), F32),
        scratch_shapes=[pltpu.VMEM((PEER_KEYS * pitch, PEER_KEYS), F32)],
        compiler_params=pltpu.CompilerParams(dimension_semantics=("parallel", "arbitrary")),
        name="peer_experts",
    )(h2, u_t, v_tab, i1, i2, gates, x1, g_final)


def _tile(n, target):
    return min(n, target)


def kernel(x_prompt, x_sample, cache_k, cache_v, state_hgrn, page_table, g_attn, w_in, hg_lb_logits,
           hg_norm, sb_bias, w_branch_a, w_branch_b, w_out, g_ffn, peer_wq, peer_subkeys, peer_u, peer_v,
           g_final):
    depth = w_in.shape[0]
    batch, seq, _ = x_prompt.shape
    nb, dec_seq, _ = x_sample.shape
    assert dec_seq == 1
    n_p = batch * seq
    xp = x_prompt.reshape(n_p, D_MODEL)
    xs = x_sample.reshape(nb, D_MODEL)
    g_fin = g_final.reshape(1, D_MODEL)
    outs = {k: [] for k in ("sp", "kp", "vp", "ss", "ks", "vs")}
    for l in range(depth):
        w_in_l = w_in[l].astype(BF16)
        wa, wb, wo = (w[l].astype(BF16) for w in (w_branch_a, w_branch_b, w_out))
        wq = peer_wq[l].astype(BF16)
        sk = peer_subkeys[l].astype(BF16)
        u_t = peer_u[l].astype(BF16).T
        v_tab = peer_v[l].astype(BF16)
        g_a = g_attn[l].reshape(1, D_MODEL)
        g_f = g_ffn[l].reshape(1, D_MODEL)
        ng = hg_norm[l].reshape(1, D_MODEL)
        bias_b = jnp.broadcast_to(sb_bias[l].astype(F32)[:, None], (N_HEADS, SB_BLOCK))
        last = l == depth - 1

        proj, k_new, v_new = _norm_proj(xp, g_a, w_in_l, _tile(n_p, 1024))
        o_a, s_p = _hgrn_prompt(proj, hg_lb_logits, ng, batch, seq, l, _tile(seq, 1024))
        o_b = _sb_prompt(proj, sb_bias[l].astype(F32), batch, seq, _tile(seq, SB_TILE))
        x1, h2 = _merge(o_a, o_b, proj, xp, wa, wb, wo, g_f, _tile(n_p, 512))
        i1, i2, gates = _peer_route(h2, wq, sk, _tile(n_p, 256))
        xp = _peer_experts(h2, u_t, v_tab, i1, i2, gates, x1, g_fin, last, _tile(n_p, 512), 1024, 2048)
        outs["sp"].append(s_p)
        outs["kp"].append(k_new.reshape(batch, seq, N_HEADS, HEAD_DIM))
        outs["vp"].append(v_new.reshape(batch, seq, N_HEADS, HEAD_DIM))

        proj_s, k_new_s, v_new_s = _norm_proj(xs, g_a, w_in_l, nb)
        o_a_s, s_s = _hgrn_step(proj_s, hg_lb_logits, hg_norm[l], state_hgrn[l], l)
        o_b_s = _sb_decode(proj_s, bias_b, cache_k[l], cache_v[l], page_table, 16)
        x1_s, h2_s = _merge(o_a_s.reshape(nb, D_MODEL), o_b_s, proj_s, xs, wa, wb, wo, g_f, nb)
        pad = (-nb) % PEER_KEYS
        sel_s = _peer_route(jnp.pad(h2_s, ((0, pad), (0, 0))), wq, sk, PEER_KEYS)
        i1_s, i2_s, gates_s = (a[:nb] for a in sel_s)
        xs = _peer_experts(h2_s, u_t, v_tab, i1_s, i2_s, gates_s, x1_s, g_fin, last, nb, 2048, 2048)
        outs["ss"].append(s_s)
        outs["ks"].append(k_new_s.reshape(nb, 1, N_HEADS, HEAD_DIM))
        outs["vs"].append(v_new_s.reshape(nb, 1, N_HEADS, HEAD_DIM))

    y_prompt = xp.reshape(batch, seq, D_MODEL)
    y_sample = xs.reshape(nb, 1, D_MODEL)
    return (y_prompt, y_sample, jnp.stack(outs["sp"]), jnp.stack(outs["kp"]), jnp.stack(outs["vp"]),
            jnp.stack(outs["ss"]), jnp.stack(outs["ks"]), jnp.stack(outs["vs"]))
```

```python
import functools

import numpy as np
import jax
import jax.numpy as jnp
from jax import lax
from jax.experimental import pallas as pl
from jax.experimental.pallas import tpu as pltpu

F32 = jnp.float32
BF16 = jnp.bfloat16

D_MODEL = 1024
N_HEADS = 8
HEAD_DIM = 128
N_GROUPS = 9
HG_CHUNK = 128
SB_BLOCK = 128
SB_TILE = 1024
SB_SPAN = 256
PEER_GROUP = 8
PEER_GROUP_UNROLL = 2
PEER_KEYS = 128
PEER_TOPK = 16
PEER_SEL = N_HEADS * PEER_TOPK
RMS_EPS = 1e-6
NEG_INF = float("-inf")
LOG2E = 1.4426950408889634

G_HQ, G_HF, G_HI, G_HG, G_SQ, G_SK, G_SV, G_GA, G_GB = range(N_GROUPS)


def _dot(a, b):
    return jnp.dot(a, b, preferred_element_type=F32)


def _dot_nt(a, b):
    return lax.dot_general(a, b, (((1,), (1,)), ((), ())), preferred_element_type=F32)


def _dot_tn(a, b):
    return lax.dot_general(a, b, (((0,), (0,)), ((), ())), preferred_element_type=F32)


def _split2(x):
    hi = x.astype(BF16)
    lo = (x - hi.astype(F32)).astype(BF16)
    return hi, lo


def _split3(x):
    hi = x.astype(BF16)
    r = x - hi.astype(F32)
    mid = r.astype(BF16)
    lo = (r - mid.astype(F32)).astype(BF16)
    return hi, mid, lo


def _exact_dot_lhs01(m01, x):
    n = x.shape[1]
    parts = _dot(m01, jnp.concatenate(_split3(x), axis=1))
    return parts[:, :n] + parts[:, n:2 * n] + parts[:, 2 * n:]


def _exact_dot_rhs01(x, m01):
    hi, mid, lo = _split3(x)
    return _dot(hi, m01) + _dot(mid, m01) + _dot(lo, m01)


def _rms(x, g):
    return x * lax.rsqrt(jnp.mean(x * x, axis=-1, keepdims=True) + RMS_EPS) * g


def _silu(x):
    return x * jax.nn.sigmoid(x)


def _norm_proj_kernel(x_ref, g_ref, w_ref, o_ref, k_ref, v_ref, h_ref):
    j = pl.program_id(1)
    tm = x_ref.shape[0]

    @pl.when(j == 0)
    def _():
        h_ref[...] = _rms(x_ref[...], g_ref[...]).astype(BF16)

    out = _dot(h_ref[...], w_ref[...])
    o_ref[0] = out

    for group, ref in ((G_SK, k_ref), (G_SV, v_ref)):
        @pl.when(j == group)
        def _():
            for h in range(N_HEADS):
                ref[pl.ds(h, tm, stride=N_HEADS), :] = out[:, h * HEAD_DIM:(h + 1) * HEAD_DIM]


def _norm_proj(x, g, w, tm):
    n = x.shape[0]
    kv_spec = pl.BlockSpec((tm * N_HEADS, HEAD_DIM), lambda i, j: (i, 0))
    kv_shape = jax.ShapeDtypeStruct((n * N_HEADS, HEAD_DIM), F32)
    return pl.pallas_call(
        _norm_proj_kernel,
        grid=(n // tm, N_GROUPS),
        in_specs=[
            pl.BlockSpec((tm, D_MODEL), lambda i, j: (i, 0)),
            pl.BlockSpec((1, D_MODEL), lambda i, j: (0, 0)),
            pl.BlockSpec((D_MODEL, D_MODEL), lambda i, j: (0, j)),
        ],
        out_specs=[pl.BlockSpec((1, tm, D_MODEL), lambda i, j: (j, i, 0)), kv_spec, kv_spec],
        out_shape=[jax.ShapeDtypeStruct((N_GROUPS, n, D_MODEL), F32), kv_shape, kv_shape],
        scratch_shapes=[pltpu.VMEM((tm, D_MODEL), BF16)],
        compiler_params=pltpu.CompilerParams(dimension_semantics=("parallel", "arbitrary")),
        name="norm_proj",
    )(x, g, w)


def _hgrn_consts():
    c = HG_CHUNK
    t = np.arange(c)[:, None]
    j = np.arange(c)[None, :]
    sel = [(j <= t)]
    masks = []
    m = c // 2
    while m >= 1:
        mid = (t // (2 * m)) * (2 * m) + m
        upper = t >= mid
        d = np.where(upper, (j >= mid) & (j <= t), (j > t) & (j <= mid - 1))
        sel.append(d)
        s = j
        same = (t // (2 * m)) == (s // (2 * m))
        s_mid = (s // (2 * m)) * (2 * m) + m
        masks.append(same & upper & (s < s_mid))
        m //= 2
    sel.append(j > t)
    masks.append(t == j)
    sel = np.concatenate(sel, axis=0).astype(np.float32)
    masks = np.stack(masks).astype(np.float32)
    return jnp.asarray(sel, dtype=BF16), jnp.asarray(masks, dtype=F32)


def _lower_bound(lbl, layer):
    e = jnp.exp(lbl - jnp.max(lbl, axis=0, keepdims=True))
    return jnp.sum(e[: layer + 1], axis=0, keepdims=True) / jnp.sum(e, axis=0, keepdims=True)


def _hgrn_kernel(q_ref, f_ref, i_ref, g_ref, lbl_ref, ng_ref, sel_ref, mask_ref,
                 o_ref, s_ref, st_ref, *, n_sub, layer):
    c = pl.program_id(2)
    n_lvl = mask_ref.shape[0] - 1
    ch = HG_CHUNK

    @pl.when(c == 0)
    def _():
        st_ref[...] = jnp.zeros_like(st_ref)

    lb = _lower_bound(lbl_ref[...], layer)
    ng = ng_ref[...]
    sel = sel_ref[...]

    chunks = []
    for sc in range(n_sub):
        rows = slice(sc * ch, (sc + 1) * ch)
        f = lb + (1.0 - lb) * jax.nn.sigmoid(f_ref[0, rows, :])
        logf = jnp.log(f)
        k = 1.0 - f
        q = _silu(q_ref[0, rows, :])
        v = i_ref[0, rows, :].astype(BF16)
        ex = jnp.exp(_exact_dot_lhs01(sel, logf))
        scores = _dot_nt(q.astype(BF16), k.astype(BF16)) * mask_ref[n_lvl]
        for li in range(n_lvl):
            el = ex[(li + 1) * ch:(li + 2) * ch]
            scores += _dot_nt((q * el).astype(BF16), (k * el).astype(BF16)) * mask_ref[li]
        e_last = ex[(n_lvl + 1) * ch:(n_lvl + 2) * ch]
        chunks.append((
            (q * ex[0:ch]).astype(BF16),
            _dot(scores.astype(BF16), v),
            ex[ch - 1:ch],
            _dot_tn(v, (k * e_last).astype(BF16)),
        ))

    st = st_ref[...]
    for sc, (q_dec, o_intra, decay, kv) in enumerate(chunks):
        rows = slice(sc * ch, (sc + 1) * ch)
        o = o_intra + _dot_nt(q_dec, st.astype(BF16))
        st = st * decay + kv
        o = o * lax.rsqrt(jnp.mean(o * o, axis=-1, keepdims=True) + RMS_EPS)
        o_ref[rows, :] = (o * ng * _silu(g_ref[0, rows, :])).astype(o_ref.dtype)
    st_ref[...] = st

    @pl.when(c == pl.num_programs(2) - 1)
    def _():
        s_ref[0, 0] = st_ref[...].T


def _hgrn_prompt(proj, lb_logits, norm_g, batch, seq, layer, tc):
    sel, masks = _hgrn_consts()
    nc = seq // tc
    n = batch * seq

    def pmap(g):
        return lambda b, h, c: (g, b * nc + c, h)

    kern = functools.partial(_hgrn_kernel, n_sub=tc // HG_CHUNK, layer=layer)
    return pl.pallas_call(
        kern,
        grid=(batch, N_HEADS, nc),
        in_specs=[
            pl.BlockSpec((1, tc, HEAD_DIM), pmap(G_HQ)),
            pl.BlockSpec((1, tc, HEAD_DIM), pmap(G_HF)),
            pl.BlockSpec((1, tc, HEAD_DIM), pmap(G_HI)),
            pl.BlockSpec((1, tc, HEAD_DIM), pmap(G_HG)),
            pl.BlockSpec((lb_logits.shape[0], HEAD_DIM), lambda b, h, c: (0, h)),
            pl.BlockSpec((1, HEAD_DIM), lambda b, h, c: (0, h)),
            pl.BlockSpec(sel.shape, lambda b, h, c: (0, 0)),
            pl.BlockSpec(masks.shape, lambda b, h, c: (0, 0, 0)),
        ],
        out_specs=[
            pl.BlockSpec((tc, HEAD_DIM), lambda b, h, c: (b * nc + c, h)),
            pl.BlockSpec((1, 1, HEAD_DIM, HEAD_DIM), lambda b, h, c: (b, h, 0, 0)),
        ],
        out_shape=[
            jax.ShapeDtypeStruct((n, D_MODEL), BF16),
            jax.ShapeDtypeStruct((batch, N_HEADS, HEAD_DIM, HEAD_DIM), F32),
        ],
        scratch_shapes=[pltpu.VMEM((HEAD_DIM, HEAD_DIM), F32)],
        compiler_params=pltpu.CompilerParams(
            dimension_semantics=("parallel", "parallel", "arbitrary")),
        name="hgrn_prompt",
    )(proj, proj, proj, proj, lb_logits, norm_g, sel, masks)


def _hgrn_step_kernel(q_ref, f_ref, i_ref, g_ref, lbl_ref, ng_ref, s_ref, o_ref, sn_ref, *, layer):
    lb = _lower_bound(lbl_ref[...], layer)
    lb = lb[0]
    f = lb + (1.0 - lb) * jax.nn.sigmoid(f_ref[0, 0])
    q = _silu(q_ref[0, 0])
    v = i_ref[0, 0]
    g = g_ref[0, 0]
    ng = ng_ref[...]
    pad = jnp.zeros((HEAD_DIM - 2 * N_HEADS, HEAD_DIM), F32)
    cols = jnp.concatenate([f, q, pad], axis=0).T
    for h in range(N_HEADS):
        fcol = cols[:, h:h + 1]
        qcol = cols[:, N_HEADS + h:N_HEADS + h + 1]
        s_new = fcol * s_ref[0, h] + (1.0 - fcol) * v[h:h + 1, :]
        sn_ref[0, h] = s_new
        o = jnp.sum(qcol * s_new, axis=0, keepdims=True)
        o = o * lax.rsqrt(jnp.mean(o * o, axis=-1, keepdims=True) + RMS_EPS)
        o_ref[0, pl.ds(h, 1), :] = o * ng[h:h + 1, :] * _silu(g[h:h + 1, :])


def _hgrn_step(proj, lb_logits, norm_g, state, layer):
    nb = state.shape[0]
    proj4 = proj.reshape(N_GROUPS, nb, N_HEADS, HEAD_DIM)
    lbl = lb_logits.reshape(lb_logits.shape[0], N_HEADS, HEAD_DIM)
    ng = norm_g.reshape(N_HEADS, HEAD_DIM)

    def pmap(g):
        return lambda b: (g, b, 0, 0)

    blk = (1, 1, N_HEADS, HEAD_DIM)
    return pl.pallas_call(
        functools.partial(_hgrn_step_kernel, layer=layer),
        grid=(nb,),
        in_specs=[
            pl.BlockSpec(blk, pmap(G_HQ)),
            pl.BlockSpec(blk, pmap(G_HF)),
            pl.BlockSpec(blk, pmap(G_HI)),
            pl.BlockSpec(blk, pmap(G_HG)),
            pl.BlockSpec(lbl.shape, lambda b: (0, 0, 0)),
            pl.BlockSpec(ng.shape, lambda b: (0, 0)),
            pl.BlockSpec((1, N_HEADS, HEAD_DIM, HEAD_DIM), lambda b: (b, 0, 0, 0)),
        ],
        out_specs=[
            pl.BlockSpec((1, N_HEADS, HEAD_DIM), lambda b: (b, 0, 0)),
            pl.BlockSpec((1, N_HEADS, HEAD_DIM, HEAD_DIM), lambda b: (b, 0, 0, 0)),
        ],
        out_shape=[
            jax.ShapeDtypeStruct((nb, N_HEADS, HEAD_DIM), F32),
            jax.ShapeDtypeStruct(state.shape, F32),
        ],
        name="hgrn_step",
    )(proj4, proj4, proj4, proj4, lbl, ng, state)


def _later_sum_consts():
    j = np.arange(SB_BLOCK)[:, None]
    k = np.arange(SB_BLOCK)[None, :]
    u = np.concatenate([(j > k), np.ones((SB_BLOCK, SB_BLOCK), bool)], axis=1)
    return jnp.asarray(u.astype(np.float32), dtype=BF16)


def _log_beta_terms(z):
    l1p = jnp.log(1.0 + jnp.exp(-jnp.abs(z)))
    return jnp.minimum(z, 0.0) - l1p, -jnp.maximum(z, 0.0) - l1p


def _strictly_later_const(span):
    j = np.arange(span)[:, None]
    k = np.arange(span)[None, :]
    return jnp.asarray((j > k).astype(np.float32), dtype=BF16)


def _mask_last_block(x, visible):
    cut = x.shape[1] - SB_BLOCK
    last = jnp.where(visible, x[:, cut:], 0.0)
    return last if cut == 0 else jnp.concatenate([x[:, :cut], last], axis=1)


def _sb_rows(q, k, v, carry, acc, later01, bias, visible):
    blk = SB_BLOCK
    span = later01.shape[0]
    n_keys = k.shape[0]
    z = _dot_nt(q, k) + bias
    neg_abs = lax.bitcast_convert_type(lax.bitcast_convert_type(z, jnp.uint32) | jnp.uint32(0x80000000), F32)
    log_b = jnp.minimum(z, 0.0) - jnp.log2(1.0 + jnp.exp2(neg_abs))
    log_keep = log_b - z
    if visible is not None:
        log_keep = _mask_last_block(log_keep, visible)
    ws = []
    for c0 in reversed(range(0, n_keys, span)):
        wd = min(span, n_keys - c0)
        lk = log_keep[:, c0:c0 + wd]
        u = later01 if wd == span else later01[:wd, :wd]
        later = _dot(lk.astype(BF16), u)
        ws.insert(0, jnp.exp2(log_b[:, c0:c0 + wd] + later + jnp.concatenate([carry] * (wd // blk), axis=1)))
        carry = carry + jnp.sum(lk, axis=1, keepdims=True)
    w = jnp.concatenate(ws, axis=1)
    if visible is not None:
        w = _mask_last_block(w, visible)
    return carry, acc + _dot(w.astype(BF16), v)


def _sb_prompt_kernel(bias_ref, q_ref, k_ref, v_ref, uo_ref, o_ref, *, tile):
    h = pl.program_id(1)
    qi = pl.program_id(2)
    blk = SB_BLOCK
    nsub = tile // blk
    q = (q_ref[0] * (LOG2E * HEAD_DIM ** -0.5)).astype(BF16)
    bias = bias_ref[h] * LOG2E
    uo = uo_ref[...]
    visible = (lax.broadcasted_iota(jnp.int32, (blk, blk), 1)
               < lax.broadcasted_iota(jnp.int32, (blk, blk), 0))

    def keys(kt):
        rows = pl.ds(pl.multiple_of(kt * tile, tile), tile)
        return k_ref[0, rows, :].astype(BF16), v_ref[0, rows, :].astype(BF16)

    k, v = keys(qi)
    zeros = jnp.zeros((blk, blk), F32)
    parts = [_sb_rows(q[r * blk:(r + 1) * blk], k[:(r + 1) * blk], v[:(r + 1) * blk],
                      zeros, zeros, uo, bias, visible) for r in range(nsub)]
    carry = jnp.concatenate([p[0] for p in parts], axis=0)
    acc = jnp.concatenate([p[1] for p in parts], axis=0)

    def body(s, ca):
        k, v = keys(qi - 1 - s)
        return _sb_rows(q, k, v, ca[0], ca[1], uo, bias, None)

    carry, acc = lax.fori_loop(0, qi, body, (carry, acc))
    o_ref[...] = acc.astype(o_ref.dtype)


def _sb_prompt(proj, bias, batch, seq, tile):
    nq = seq // tile
    n = batch * seq
    uo = _strictly_later_const(min(tile, SB_SPAN))
    return pl.pallas_call(
        functools.partial(_sb_prompt_kernel, tile=tile),
        grid=(batch, N_HEADS, nq),
        in_specs=[
            pl.BlockSpec(memory_space=pltpu.SMEM),
            pl.BlockSpec((1, tile, HEAD_DIM), lambda b, h, i: (G_SQ, b * nq + i, h)),
            pl.BlockSpec((1, seq, HEAD_DIM), lambda b, h, i: (G_SK, b, h)),
            pl.BlockSpec((1, seq, HEAD_DIM), lambda b, h, i: (G_SV, b, h)),
            pl.BlockSpec(uo.shape, lambda b, h, i: (0, 0)),
        ],
        out_specs=pl.BlockSpec((tile, HEAD_DIM), lambda b, h, i: (b * nq + i, h)),
        out_shape=jax.ShapeDtypeStruct((n, D_MODEL), BF16),
        compiler_params=pltpu.CompilerParams(
            dimension_semantics=("parallel", "parallel", "arbitrary")),
        name="sb_prompt",
    )(bias, proj, proj, proj, uo)


def _sb_decode_kernel(pt_ref, q_ref, bias_ref, uo_ref, *refs, pages_per_step):
    k_refs = refs[:pages_per_step]
    v_refs = refs[pages_per_step:2 * pages_per_step]
    o_ref, carry_ref, acc_ref = refs[2 * pages_per_step:]
    s = pl.program_id(1)
    scale = HEAD_DIM ** -0.5
    blk = SB_BLOCK
    span = pages_per_step * blk
    head_of_row = lax.broadcasted_iota(jnp.int32, (N_HEADS, span), 0)

    @pl.when(s == 0)
    def _():
        carry_ref[...] = jnp.zeros_like(carry_ref)
        acc_ref[...] = jnp.zeros_like(acc_ref)

    def head_rows(refs, h):
        return jnp.concatenate(
            [ref[0, pl.ds(h, blk, stride=N_HEADS), :] for ref in refs], axis=0).astype(BF16)

    q = q_ref[0, 0].astype(BF16)
    z = jnp.zeros((N_HEADS, span), F32)
    for h in range(N_HEADS):
        z = jnp.where(head_of_row == h, _dot_nt(q, head_rows(k_refs, h)), z)
    z = z * scale + jnp.concatenate([bias_ref[...]] * pages_per_step, axis=1)
    log_b, log_keep = _log_beta_terms(z)
    lk_rows = jnp.concatenate([log_keep[:, r * blk:(r + 1) * blk] for r in range(pages_per_step)], axis=0)
    cs = _exact_dot_rhs01(lk_rows, uo_ref[...])
    carry = carry_ref[...]
    ws = []
    for r in range(pages_per_step):
        rows = slice(r * N_HEADS, (r + 1) * N_HEADS)
        ws.append(jnp.exp(log_b[:, r * blk:(r + 1) * blk] + cs[rows, :blk] + carry).astype(BF16))
        carry = carry + cs[rows, blk:]
    carry_ref[...] = carry
    w = jnp.concatenate(ws, axis=1)
    for h in range(N_HEADS):
        acc_ref[h] += _dot(w, head_rows(v_refs, h))

    @pl.when(s == pl.num_programs(1) - 1)
    def _():
        out = jnp.zeros((N_HEADS, HEAD_DIM), F32)
        for h in range(N_HEADS):
            out = jnp.where(head_of_row[:, :HEAD_DIM] == h, acc_ref[h], out)
        o_ref[0] = out


def _sb_decode(proj, bias_b, cache_k, cache_v, page_table, pages_per_step):
    nb, n_pages = page_table.shape
    n_pool, page = cache_k.shape[0], cache_k.shape[1]
    assert page == SB_BLOCK
    ck = cache_k.reshape(n_pool, page * N_HEADS, HEAD_DIM)
    cv = cache_v.reshape(n_pool, page * N_HEADS, HEAD_DIM)
    proj4 = proj.reshape(N_GROUPS, nb, N_HEADS, HEAD_DIM)
    uo = _later_sum_consts()
    pt = page_table.reshape(-1)

    def page_map(r):
        return lambda b, s, pt: (pt[b * n_pages + n_pages - 1 - (s * pages_per_step + r)], 0, 0)

    def page_specs():
        return [pl.BlockSpec((1, page * N_HEADS, HEAD_DIM), page_map(r)) for r in range(pages_per_step)]

    grid_spec = pltpu.PrefetchScalarGridSpec(
        num_scalar_prefetch=1,
        grid=(nb, n_pages // pages_per_step),
        in_specs=[
            pl.BlockSpec((1, 1, N_HEADS, HEAD_DIM), lambda b, s, pt: (G_SQ, b, 0, 0)),
            pl.BlockSpec(bias_b.shape, lambda b, s, pt: (0, 0)),
            pl.BlockSpec(uo.shape, lambda b, s, pt: (0, 0)),
        ] + page_specs() + page_specs(),
        out_specs=pl.BlockSpec((1, N_HEADS, HEAD_DIM), lambda b, s, pt: (b, 0, 0)),
        scratch_shapes=[pltpu.VMEM((N_HEADS, SB_BLOCK), F32),
                        pltpu.VMEM((N_HEADS, N_HEADS, HEAD_DIM), F32)],
    )
    out = pl.pallas_call(
        functools.partial(_sb_decode_kernel, pages_per_step=pages_per_step),
        grid_spec=grid_spec,
        out_shape=jax.ShapeDtypeStruct((nb, N_HEADS, HEAD_DIM), F32),
        compiler_params=pltpu.CompilerParams(dimension_semantics=("parallel", "arbitrary")),
        name="sb_decode",
    )(pt, proj4, bias_b, uo, *([ck] * pages_per_step), *([cv] * pages_per_step))
    return out.reshape(nb, D_MODEL)


def _merge_kernel(oa_ref, ob_ref, ga_ref, gb_ref, x_ref, wa_ref, wb_ref, wo_ref, gf_ref,
                  x1_ref, h2_ref):
    a = _dot(oa_ref[...].astype(BF16), wa_ref[...])
    b = _dot(ob_ref[...].astype(BF16), wb_ref[...])
    m = jax.nn.sigmoid(ga_ref[0]) * a + jax.nn.sigmoid(gb_ref[0]) * b
    x1 = x_ref[...] + _dot(m.astype(BF16), wo_ref[...])
    x1_ref[...] = x1
    h2_ref[...] = _rms(x1, gf_ref[...]).astype(BF16)


def _merge(o_a, o_b, proj, x, wa, wb, wo, g_ffn, tm):
    n = x.shape[0]
    row = pl.BlockSpec((tm, D_MODEL), lambda i: (i, 0))
    wspec = pl.BlockSpec((D_MODEL, D_MODEL), lambda i: (0, 0))
    return pl.pallas_call(
        _merge_kernel,
        grid=(n // tm,),
        in_specs=[
            row, row,
            pl.BlockSpec((1, tm, D_MODEL), lambda i: (G_GA, i, 0)),
            pl.BlockSpec((1, tm, D_MODEL), lambda i: (G_GB, i, 0)),
            row, wspec, wspec, wspec,
            pl.BlockSpec((1, D_MODEL), lambda i: (0, 0)),
        ],
        out_specs=[row, row],
        out_shape=[jax.ShapeDtypeStruct((n, D_MODEL), F32), jax.ShapeDtypeStruct((n, D_MODEL), BF16)],
        compiler_params=pltpu.CompilerParams(dimension_semantics=("parallel",)),
        name="merge",
    )(o_a, o_b, proj, proj, x, wa, wb, wo, g_ffn)


def _top_rows(vals, order, payload, n):
    out_v, out_p = [], []
    for _ in range(n):
        m = jnp.max(vals, axis=0, keepdims=True)
        first = jnp.min(jnp.where(vals == m, order, 1e9), axis=0, keepdims=True)
        hit = order == first
        out_v.append(m)
        if payload is None:
            out_p.append(first)
        else:
            out_p.append(jnp.max(jnp.where(hit, payload, -1.0), axis=0, keepdims=True))
        vals = jnp.where(hit, NEG_INF, vals)
    return out_v, out_p


def _stack_rows(rows, sub):
    out = jnp.zeros(sub.shape, F32)
    for r, row in enumerate(rows):
        out = jnp.where(sub == r, row, out)
    return out


def _pair_candidates():
    pairs = [(0, b) for b in range(16)] + [(1, b) for b in range(8)]
    pairs += [(2, b) for b in range(5)] + [(4, b) for b in range(3)]
    pairs += [(3, b) for b in range(4)] + [(5, 0), (5, 1), (6, 0), (6, 1)]
    pairs += [(7, 0), (7, 1)] + [(a, 0) for a in range(8, 16)]
    assert sorted(pairs) == sorted((a, b) for a in range(16) for b in range(16) if (a + 1) * (b + 1) <= 16)
    return pairs + [(-1, 0)] * (-len(pairs) % 8)


def _peer_route_kernel(h_ref, wq_ref, sk_ref, order_ref, i1_ref, i2_ref, gate_ref):
    t = h_ref.shape[0]
    q = _dot(h_ref[...], wq_ref[...]).astype(BF16)
    key_iota = lax.broadcasted_iota(jnp.int32, (PEER_KEYS, t), 0).astype(F32)
    sub16 = lax.broadcasted_iota(jnp.int32, (PEER_TOPK, t), 0)
    sub8 = lax.broadcasted_iota(jnp.int32, (8, t), 0)
    pairs = _pair_candidates()
    order = order_ref[...]
    k1_rows, k2_rows, g_rows = [], [], []
    for h in range(N_HEADS):
        top_v, top_i = [], []
        for p in range(2):
            c0 = (h * 2 + p) * PEER_KEYS
            s = _dot_nt(sk_ref[p], q[:, c0:c0 + PEER_KEYS])
            v, i = _top_rows(s, key_iota, None, PEER_TOPK)
            top_v.append(v)
            top_i.append(i)
        cand = []
        for r0 in range(0, len(pairs), 8):
            tile = pairs[r0:r0 + 8]
            va = _stack_rows([top_v[0][max(a, 0)] for a, _ in tile], sub8)
            vb = _stack_rows([top_v[1][b] for _, b in tile], sub8)
            cand.append(va + vb)
        cand = jnp.where(order < 1e6, jnp.concatenate(cand, axis=0), NEG_INF)
        best, flat = _top_rows(cand, order, None, PEER_TOPK)
        flat = _stack_rows(flat, sub16)
        rank_a = jnp.floor(flat * (1.0 / PEER_TOPK))
        rank_b = flat - rank_a * float(PEER_TOPK)
        key1 = jnp.zeros_like(flat)
        key2 = jnp.zeros_like(flat)
        for r in range(PEER_TOPK):
            key1 = jnp.where(rank_a == float(r), top_i[0][r], key1)
            key2 = jnp.where(rank_b == float(r), top_i[1][r], key2)
        top = best[0]
        best = _stack_rows(best, sub16)
        ex = jnp.exp(best - top)
        g_rows.append(ex / jnp.sum(ex, axis=0, keepdims=True))
        k1_rows.append(key1)
        k2_rows.append(key2)
    i1_ref[...] = jnp.concatenate(k1_rows, axis=0).T
    i2_ref[...] = jnp.concatenate(k2_rows, axis=0).T
    gate_ref[...] = jnp.concatenate(g_rows, axis=0).T


def _peer_route(h2, wq, sub_keys, tt):
    n = h2.shape[0]
    sel = pl.BlockSpec((tt, PEER_SEL), lambda i: (i, 0))
    shp = jax.ShapeDtypeStruct((n, PEER_SEL), F32)
    flat = np.array([a * PEER_TOPK + b if a >= 0 else 1e9 for a, b in _pair_candidates()], np.float32)
    order = jnp.asarray(np.broadcast_to(flat[:, None], (flat.shape[0], tt)))
    return pl.pallas_call(
        _peer_route_kernel,
        grid=(n // tt,),
        in_specs=[
            pl.BlockSpec((tt, D_MODEL), lambda i: (i, 0)),
            pl.BlockSpec(wq.shape, lambda i: (0, 0)),
            pl.BlockSpec(sub_keys.shape, lambda i: (0, 0, 0)),
            pl.BlockSpec(order.shape, lambda i: (0, 0)),
        ],
        out_specs=[sel, sel, sel],
        out_shape=[shp, shp, shp],
        compiler_params=pltpu.CompilerParams(dimension_semantics=("parallel",)),
        name="peer_route",
    )(h2, wq, sub_keys, order)


def _slab_pitch(tt):
    p = tt // 8
    return 8 * (p + 1 if p % 2 == 0 else p + 2)


def _peer_expert_kernel(h_ref, ut_ref, v_ref, i1_ref, i2_ref, gate_ref, x_ref, gf_ref,
                        y_ref, slab_ref, *, n_up, n_down, pitch, final_norm):
    s = pl.program_id(1)
    tt = h_ref.shape[0]
    slabs_up = ut_ref.shape[1] // PEER_KEYS
    slabs_down = v_ref.shape[0] // PEER_KEYS
    key_iota = lax.broadcasted_iota(jnp.int32, (PEER_KEYS, PEER_SEL), 0).astype(F32)

    def slab_rows(g):
        return pl.ds(pl.multiple_of(g * pitch, 8), tt)

    def token_rows(t):
        return pl.ds(t, PEER_KEYS, stride=pitch)

    @pl.when(s < n_up)
    def _():
        a = _dot(h_ref[...], ut_ref[...])
        for gl in range(slabs_up):
            slab_ref[slab_rows(s * slabs_up + gl), :] = a[:, gl * PEER_KEYS:(gl + 1) * PEER_KEYS]

    @pl.when(s == n_up - 1)
    def _():
        sub = lax.broadcasted_iota(jnp.int32, (PEER_GROUP, PEER_SEL), 0)

        def group_base(gi):
            base = gi * PEER_GROUP
            return base if isinstance(base, int) else pl.multiple_of(base, PEER_GROUP)

        def hits(t):
            return (key_iota == i1_ref[pl.ds(t, 1), :],
                    (key_iota == i2_ref[pl.ds(t, 1), :]).astype(BF16))

        def select(gi):
            base = group_base(gi)
            rows = []
            for u in range(PEER_GROUP):
                t = base + u
                hit1, hit2 = hits(t)
                pre = slab_ref[token_rows(t), :]
                col = jnp.broadcast_to(i2_ref[pl.ds(t, 1), :].astype(jnp.int32), (PEER_KEYS, PEER_SEL))
                picked = jnp.take_along_axis(pre, col, axis=1)
                rows.append(jnp.sum(jnp.where(hit1, picked, 0.0), axis=0, keepdims=True))
            return _stack_rows(rows, sub)

        def scatter(gi, pre):
            base = group_base(gi)
            wv = gate_ref[pl.ds(base, PEER_GROUP), :] * (0.5 * pre * (1.0 + lax.erf(pre * (0.5 ** 0.5))))
            for u in range(PEER_GROUP):
                hit1, hit2 = hits(base + u)
                w1 = jnp.where(hit1, wv[u:u + 1, :], 0.0).astype(BF16)
                slab_ref[token_rows(base + u), :] = _dot_nt(w1, hit2)

        def step(gi, pre_prev):
            pre = select(gi)
            scatter(gi - 1, pre_prev)
            return pre

        n_groups = tt // PEER_GROUP
        pre_last = lax.fori_loop(1, n_groups, step, select(0), unroll=PEER_GROUP_UNROLL)
        scatter(n_groups - 1, pre_last)
        y_ref[...] = jnp.zeros_like(y_ref)

    @pl.when(s >= n_up)
    def _():
        c = s - n_up
        w = jnp.concatenate(
            [slab_ref[slab_rows(c * slabs_down + gl), :].astype(BF16) for gl in range(slabs_down)], axis=1)
        y_ref[...] += _dot(w, v_ref[...])

    @pl.when(s == n_up + n_down - 1)
    def _():
        x2 = x_ref[...] + y_ref[...]
        y_ref[...] = _rms(x2, gf_ref[...]) if final_norm else x2


def _peer_experts(h2, u_t, v_tab, i1, i2, gates, x1, g_final, final_norm, tt, chunk_up, chunk_down):
    n = h2.shape[0]
    n_exp = v_tab.shape[0]
    n_up, n_down = n_exp // chunk_up, n_exp // chunk_down
    pitch = _slab_pitch(tt)
    row = pl.BlockSpec((tt, D_MODEL), lambda i, s: (i, 0))
    sel = pl.BlockSpec((tt, PEER_SEL), lambda i, s: (i, 0))
    return pl.pallas_call(
        functools.partial(_peer_expert_kernel, n_up=n_up, n_down=n_down, pitch=pitch, final_norm=final_norm),
        grid=(n // tt, n_up + n_down),
        in_specs=[
            row,
            pl.BlockSpec((D_MODEL, chunk_up), lambda i, s: (0, jnp.minimum(s, n_up - 1))),
            pl.BlockSpec((chunk_down, D_MODEL), lambda i, s: (jnp.maximum(s - n_up, 0), 0)),
            sel, sel, sel, row,
            pl.BlockSpec((1, D_MODEL), lambda i, s: (0, 0)),
        ],
        out_specs=row,
        out_shape=jax.ShapeDtypeStruct((n, D_MODEL), F32),
        scratch_shapes=[pltpu.VMEM((PEER_KEYS * pitch, PEER_KEYS), F32)],
        compiler_params=pltpu.CompilerParams(dimension_semantics=("parallel", "arbitrary")),
        name="peer_experts",
    )(h2, u_t, v_tab, i1, i2, gates, x1, g_final)


def _tile(n, target):
    return min(n, target)


def kernel(x_prompt, x_sample, cache_k, cache_v, state_hgrn, page_table, g_attn, w_in, hg_lb_logits,
           hg_norm, sb_bias, w_branch_a, w_branch_b, w_out, g_ffn, peer_wq, peer_subkeys, peer_u, peer_v,
           g_final):
    depth = w_in.shape[0]
    batch, seq, _ = x_prompt.shape
    nb, dec_seq, _ = x_sample.shape
    assert dec_seq == 1
    n_p = batch * seq
    xp = x_prompt.reshape(n_p, D_MODEL)
    xs = x_sample.reshape(nb, D_MODEL)
    g_fin = g_final.reshape(1, D_MODEL)
    outs = {k: [] for k in ("sp", "kp", "vp", "ss", "ks", "vs")}
    for l in range(depth):
        w_in_l = w_in[l].astype(BF16)
        wa, wb, wo = (w[l].astype(BF16) for w in (w_branch_a, w_branch_b, w_out))
        wq = peer_wq[l].astype(BF16)
        sk = peer_subkeys[l].astype(BF16)
        u_t = peer_u[l].astype(BF16).T
        v_tab = peer_v[l].astype(BF16)
        g_a = g_attn[l].reshape(1, D_MODEL)
        g_f = g_ffn[l].reshape(1, D_MODEL)
        ng = hg_norm[l].reshape(1, D_MODEL)
        bias_b = jnp.broadcast_to(sb_bias[l].astype(F32)[:, None], (N_HEADS, SB_BLOCK))
        last = l == depth - 1

        proj, k_new, v_new = _norm_proj(xp, g_a, w_in_l, _tile(n_p, 1024))
        o_a, s_p = _hgrn_prompt(proj, hg_lb_logits, ng, batch, seq, l, _tile(seq, 1024))
        o_b = _sb_prompt(proj, sb_bias[l].astype(F32), batch, seq, _tile(seq, SB_TILE))
        x1, h2 = _merge(o_a, o_b, proj, xp, wa, wb, wo, g_f, _tile(n_p, 512))
        i1, i2, gates = _peer_route(h2, wq, sk, _tile(n_p, 256))
        xp = _peer_experts(h2, u_t, v_tab, i1, i2, gates, x1, g_fin, last, _tile(n_p, 512), 1024, 2048)
        outs["sp"].append(s_p)
        outs["kp"].append(k_new.reshape(batch, seq, N_HEADS, HEAD_DIM))
        outs["vp"].append(v_new.reshape(batch, seq, N_HEADS, HEAD_DIM))

        proj_s, k_new_s, v_new_s = _norm_proj(xs, g_a, w_in_l, nb)
        o_a_s, s_s = _hgrn_step(proj_s, hg_lb_logits, hg_norm[l], state_hgrn[l], l)
        o_b_s = _sb_decode(proj_s, bias_b, cache_k[l], cache_v[l], page_table, 16)
        x1_s, h2_s = _merge(o_a_s.reshape(nb, D_MODEL), o_b_s, proj_s, xs, wa, wb, wo, g_f, nb)
        pad = (-nb) % PEER_KEYS
        sel_s = _peer_route(jnp.pad(h2_s, ((0, pad), (0, 0))), wq, sk, PEER_KEYS)
        i1_s, i2_s, gates_s = (a[:nb] for a in sel_s)
        xs = _peer_experts(h2_s, u_t, v_tab, i1_s, i2_s, gates_s, x1_s, g_fin, last, nb, 2048, 2048)
        outs["ss"].append(s_s)
        outs["ks"].append(k_new_s.reshape(nb, 1, N_HEADS, HEAD_DIM))
        outs["vs"].append(v_new_s.reshape(nb, 1, N_HEADS, HEAD_DIM))

    y_prompt = xp.reshape(batch, seq, D_MODEL)
    y_sample = xs.reshape(nb, 1, D_MODEL)
    return (y_prompt, y_sample, jnp.stack(outs["sp"]), jnp.stack(outs["kp"]), jnp.stack(outs["vp"]),
            jnp.stack(outs["ss"]), jnp.stack(outs["ks"]), jnp.stack(outs["vs"]))
```

```python
import functools

import numpy as np
import jax
import jax.numpy as jnp
from jax import lax
from jax.experimental import pallas as pl
from jax.experimental.pallas import tpu as pltpu

F32 = jnp.float32
BF16 = jnp.bfloat16

D_MODEL = 1024
N_HEADS = 8
HEAD_DIM = 128
N_GROUPS = 9
SUBLANES = 8
HG_CHUNK = 128
HG_STEP_TOKENS = 1024
PROJ_ROWS = 1024
MERGE_ROWS = 512
ROUTE_TOKENS = 256
EXPERT_TOKENS = 512
EXPERT_CHUNK_UP = 1024
EXPERT_CHUNK_DOWN = 2048
DECODE_PAGES = 16
SB_BLOCK = 128
SB_TILE = 1024
SB_SPAN = 256
PEER_GROUP = 8
PEER_GROUP_UNROLL = 2
PEER_KEYS = 128
PEER_TOPK = 16
PEER_SEL = N_HEADS * PEER_TOPK
RMS_EPS = 1e-6
NEG_INF = float("-inf")
LOG2E = 1.4426950408889634

G_HQ, G_HF, G_HI, G_HG, G_SQ, G_SK, G_SV, G_GA, G_GB = range(N_GROUPS)


def _dot(a, b):
    return jnp.dot(a, b, preferred_element_type=F32)


def _dot_nt(a, b):
    return lax.dot_general(a, b, (((1,), (1,)), ((), ())), preferred_element_type=F32)


def _dot_tn(a, b):
    return lax.dot_general(a, b, (((0,), (0,)), ((), ())), preferred_element_type=F32)


def _split3(x):
    hi = x.astype(BF16)
    r = x - hi.astype(F32)
    mid = r.astype(BF16)
    lo = (r - mid.astype(F32)).astype(BF16)
    return hi, mid, lo


def _exact_dot_lhs01(m01, x):
    n = x.shape[1]
    parts = _dot(m01, jnp.concatenate(_split3(x), axis=1))
    return parts[:, :n] + parts[:, n:2 * n] + parts[:, 2 * n:]


def _exact_dot_rhs01(x, m01):
    hi, mid, lo = _split3(x)
    return _dot(hi, m01) + _dot(mid, m01) + _dot(lo, m01)


def _rms(x, g):
    return x * lax.rsqrt(jnp.mean(x * x, axis=-1, keepdims=True) + RMS_EPS) * g


def _silu(x):
    return x * jax.nn.sigmoid(x)


def _norm_proj_kernel(x_ref, g_ref, w_ref, o_ref, k_ref, v_ref, h_ref):
    j = pl.program_id(1)
    tm = x_ref.shape[0]

    @pl.when(j == 0)
    def _():
        h_ref[...] = _rms(x_ref[...], g_ref[...]).astype(BF16)

    out = _dot(h_ref[...], w_ref[...])
    o_ref[0] = out

    for group, ref in ((G_SK, k_ref), (G_SV, v_ref)):
        @pl.when(j == group)
        def _():
            for h in range(N_HEADS):
                ref[pl.ds(h, tm, stride=N_HEADS), :] = out[:, h * HEAD_DIM:(h + 1) * HEAD_DIM]


def _norm_proj(x, g, w, tm):
    n = x.shape[0]
    kv_spec = pl.BlockSpec((tm * N_HEADS, HEAD_DIM), lambda i, j: (i, 0))
    kv_shape = jax.ShapeDtypeStruct((n * N_HEADS, HEAD_DIM), F32)
    return pl.pallas_call(
        _norm_proj_kernel,
        grid=(n // tm, N_GROUPS),
        in_specs=[
            pl.BlockSpec((tm, D_MODEL), lambda i, j: (i, 0)),
            pl.BlockSpec((1, D_MODEL), lambda i, j: (0, 0)),
            pl.BlockSpec((D_MODEL, D_MODEL), lambda i, j: (0, j)),
        ],
        out_specs=[pl.BlockSpec((1, tm, D_MODEL), lambda i, j: (j, i, 0)), kv_spec, kv_spec],
        out_shape=[jax.ShapeDtypeStruct((N_GROUPS, n, D_MODEL), F32), kv_shape, kv_shape],
        scratch_shapes=[pltpu.VMEM((tm, D_MODEL), BF16)],
        compiler_params=pltpu.CompilerParams(dimension_semantics=("parallel", "arbitrary")),
        name="norm_proj",
    )(x, g, w)


def _hgrn_consts():
    c = HG_CHUNK
    t = np.arange(c)[:, None]
    j = np.arange(c)[None, :]
    sel = [(j <= t)]
    masks = []
    m = c // 2
    while m >= 1:
        mid = (t // (2 * m)) * (2 * m) + m
        upper = t >= mid
        if m < SUBLANES:
            sel.append(np.where(upper, (j >= mid) & (j <= t), (j > t) & (j <= mid - 1)))
        s = j
        same = (t // (2 * m)) == (s // (2 * m))
        s_mid = (s // (2 * m)) * (2 * m) + m
        masks.append(same & upper & (s < s_mid))
        m //= 2
    masks.append(t == j)
    sel = np.concatenate(sel, axis=0).astype(np.float32)
    masks = np.stack(masks).astype(np.float32)
    return jnp.asarray(sel, dtype=BF16), jnp.asarray(masks, dtype=F32)


def _lower_bound(lbl, layer):
    e = jnp.exp(lbl - jnp.max(lbl, axis=0, keepdims=True))
    return jnp.sum(e[: layer + 1], axis=0, keepdims=True) / jnp.sum(e, axis=0, keepdims=True)


def _level_exponent(cum, m):
    parts = []
    for b0 in range(0, cum.shape[0], 2 * m):
        ref = cum[b0 + m - 1:b0 + m]
        parts += [ref - cum[b0:b0 + m], cum[b0 + m:b0 + 2 * m] - ref]
    return jnp.concatenate(parts, axis=0)


def _hgrn_kernel(q_ref, f_ref, i_ref, g_ref, lbl_ref, ng_ref, sel_ref, mask_ref,
                 o_ref, s_ref, st_ref, *, n_sub, layer):
    c = pl.program_id(2)
    n_lvl = mask_ref.shape[0] - 1
    ch = HG_CHUNK
    halves = [ch >> (li + 1) for li in range(n_lvl)]

    @pl.when(c == 0)
    def _():
        st_ref[...] = jnp.zeros_like(st_ref)

    lb = _lower_bound(lbl_ref[...], layer)
    ng = ng_ref[...]
    sel = sel_ref[...]

    chunks = []
    for sc in range(n_sub):
        rows = slice(sc * ch, (sc + 1) * ch)
        f = lb + (1.0 - lb) * jax.nn.sigmoid(f_ref[0, rows, :])
        logf = jnp.log(f)
        k = 1.0 - f
        q = _silu(q_ref[0, rows, :])
        v = i_ref[0, rows, :].astype(BF16)
        sums = _exact_dot_lhs01(sel, logf)
        cum = sums[0:ch]
        scores = _dot_nt(q.astype(BF16), k.astype(BF16)) * mask_ref[n_lvl]
        small = 0
        for li, m in enumerate(halves):
            if m >= SUBLANES:
                el = jnp.exp(_level_exponent(cum, m))
            else:
                small += 1
                el = jnp.exp(sums[small * ch:(small + 1) * ch])
            scores += _dot_nt((q * el).astype(BF16), (k * el).astype(BF16)) * mask_ref[li]
        e_cum = jnp.exp(cum)
        e_last = jnp.exp(cum[ch - 1:ch] - cum)
        chunks.append((
            (q * e_cum).astype(BF16),
            _dot(scores.astype(BF16), v),
            e_cum[ch - 1:ch],
            _dot_tn(v, (k * e_last).astype(BF16)),
        ))

    st = st_ref[...]
    for sc, (q_dec, o_intra, decay, kv) in enumerate(chunks):
        rows = slice(sc * ch, (sc + 1) * ch)
        o = o_intra + _dot_nt(q_dec, st.astype(BF16))
        st = st * decay + kv
        o = o * lax.rsqrt(jnp.mean(o * o, axis=-1, keepdims=True) + RMS_EPS)
        o_ref[rows, :] = (o * ng * _silu(g_ref[0, rows, :])).astype(o_ref.dtype)
    st_ref[...] = st

    @pl.when(c == pl.num_programs(2) - 1)
    def _():
        s_ref[0, 0] = st_ref[...].T


def _hgrn_prompt(proj, lb_logits, norm_g, batch, seq, layer, tc):
    sel, masks = _hgrn_consts()
    nc = seq // tc
    n = batch * seq

    def pmap(g):
        return lambda b, h, c: (g, b * nc + c, h)

    kern = functools.partial(_hgrn_kernel, n_sub=tc // HG_CHUNK, layer=layer)
    return pl.pallas_call(
        kern,
        grid=(batch, N_HEADS, nc),
        in_specs=[
            pl.BlockSpec((1, tc, HEAD_DIM), pmap(G_HQ)),
            pl.BlockSpec((1, tc, HEAD_DIM), pmap(G_HF)),
            pl.BlockSpec((1, tc, HEAD_DIM), pmap(G_HI)),
            pl.BlockSpec((1, tc, HEAD_DIM), pmap(G_HG)),
            pl.BlockSpec((lb_logits.shape[0], HEAD_DIM), lambda b, h, c: (0, h)),
            pl.BlockSpec((1, HEAD_DIM), lambda b, h, c: (0, h)),
            pl.BlockSpec(sel.shape, lambda b, h, c: (0, 0)),
            pl.BlockSpec(masks.shape, lambda b, h, c: (0, 0, 0)),
        ],
        out_specs=[
            pl.BlockSpec((tc, HEAD_DIM), lambda b, h, c: (b * nc + c, h)),
            pl.BlockSpec((1, 1, HEAD_DIM, HEAD_DIM), lambda b, h, c: (b, h, 0, 0)),
        ],
        out_shape=[
            jax.ShapeDtypeStruct((n, D_MODEL), BF16),
            jax.ShapeDtypeStruct((batch, N_HEADS, HEAD_DIM, HEAD_DIM), F32),
        ],
        scratch_shapes=[pltpu.VMEM((HEAD_DIM, HEAD_DIM), F32)],
        compiler_params=pltpu.CompilerParams(
            dimension_semantics=("parallel", "parallel", "arbitrary")),
        name="hgrn_prompt",
    )(proj, proj, proj, proj, lb_logits, norm_g, sel, masks)


def _hgrn_step_kernel(q_ref, f_ref, i_ref, g_ref, lbl_ref, ng_ref, s_ref, o_ref, sn_ref, *, layer):
    lb = _lower_bound(lbl_ref[...], layer)
    lb = lb[0]
    f = lb + (1.0 - lb) * jax.nn.sigmoid(f_ref[0, 0])
    q = _silu(q_ref[0, 0])
    v = i_ref[0, 0]
    g = g_ref[0, 0]
    ng = ng_ref[...]
    pad = jnp.zeros((HEAD_DIM - 2 * N_HEADS, HEAD_DIM), F32)
    cols = jnp.concatenate([f, q, pad], axis=0).T
    for h in range(N_HEADS):
        fcol = cols[:, h:h + 1]
        qcol = cols[:, N_HEADS + h:N_HEADS + h + 1]
        s_new = fcol * s_ref[0, h] + (1.0 - fcol) * v[h:h + 1, :]
        sn_ref[0, h] = s_new
        o = jnp.sum(qcol * s_new, axis=0, keepdims=True)
        o = o * lax.rsqrt(jnp.mean(o * o, axis=-1, keepdims=True) + RMS_EPS)
        o_ref[0, pl.ds(h, 1), :] = o * ng[h:h + 1, :] * _silu(g[h:h + 1, :])


def _hgrn_step(proj, lb_logits, norm_g, state, layer):
    nb = state.shape[0]
    proj4 = proj.reshape(N_GROUPS, nb, N_HEADS, HEAD_DIM)
    lbl = lb_logits.reshape(lb_logits.shape[0], N_HEADS, HEAD_DIM)
    ng = norm_g.reshape(N_HEADS, HEAD_DIM)

    def pmap(g):
        return lambda b: (g, b, 0, 0)

    blk = (1, 1, N_HEADS, HEAD_DIM)
    return pl.pallas_call(
        functools.partial(_hgrn_step_kernel, layer=layer),
        grid=(nb,),
        in_specs=[
            pl.BlockSpec(blk, pmap(G_HQ)),
            pl.BlockSpec(blk, pmap(G_HF)),
            pl.BlockSpec(blk, pmap(G_HI)),
            pl.BlockSpec(blk, pmap(G_HG)),
            pl.BlockSpec(lbl.shape, lambda b: (0, 0, 0)),
            pl.BlockSpec(ng.shape, lambda b: (0, 0)),
            pl.BlockSpec((1, N_HEADS, HEAD_DIM, HEAD_DIM), lambda b: (b, 0, 0, 0)),
        ],
        out_specs=[
            pl.BlockSpec((1, N_HEADS, HEAD_DIM), lambda b: (b, 0, 0)),
            pl.BlockSpec((1, N_HEADS, HEAD_DIM, HEAD_DIM), lambda b: (b, 0, 0, 0)),
        ],
        out_shape=[
            jax.ShapeDtypeStruct((nb, N_HEADS, HEAD_DIM), F32),
            jax.ShapeDtypeStruct(state.shape, F32),
        ],
        name="hgrn_step",
    )(proj4, proj4, proj4, proj4, lbl, ng, state)


def _later_sum_consts():
    j = np.arange(SB_BLOCK)[:, None]
    k = np.arange(SB_BLOCK)[None, :]
    u = np.concatenate([(j > k), np.ones((SB_BLOCK, SB_BLOCK), bool)], axis=1)
    return jnp.asarray(u.astype(np.float32), dtype=BF16)


def _log_beta_terms(z):
    l1p = jnp.log(1.0 + jnp.exp(-jnp.abs(z)))
    return jnp.minimum(z, 0.0) - l1p, -jnp.maximum(z, 0.0) - l1p


def _strictly_later_const(span):
    j = np.arange(span)[:, None]
    k = np.arange(span)[None, :]
    return jnp.asarray((j > k).astype(np.float32), dtype=BF16)


def _mask_last_block(x, visible):
    cut = x.shape[1] - SB_BLOCK
    last = jnp.where(visible, x[:, cut:], 0.0)
    return last if cut == 0 else jnp.concatenate([x[:, :cut], last], axis=1)


def _sb_rows(q, k, v, carry, acc, later01, bias, visible):
    blk = SB_BLOCK
    span = later01.shape[0]
    n_keys = k.shape[0]
    z = _dot_nt(q, k) + bias
    neg_abs = lax.bitcast_convert_type(lax.bitcast_convert_type(z, jnp.uint32) | jnp.uint32(0x80000000), F32)
    log_b = jnp.minimum(z, 0.0) - jnp.log2(1.0 + jnp.exp2(neg_abs))
    log_keep = log_b - z
    if visible is not None:
        log_keep = _mask_last_block(log_keep, visible)
    ws = []
    for c0 in reversed(range(0, n_keys, span)):
        wd = min(span, n_keys - c0)
        lk = log_keep[:, c0:c0 + wd]
        u = later01 if wd == span else later01[:wd, :wd]
        later = _dot(lk.astype(BF16), u)
        ws.insert(0, jnp.exp2(log_b[:, c0:c0 + wd] + later + jnp.concatenate([carry] * (wd // blk), axis=1)))
        carry = carry + jnp.sum(lk, axis=1, keepdims=True)
    w = jnp.concatenate(ws, axis=1)
    if visible is not None:
        w = _mask_last_block(w, visible)
    return carry, acc + _dot(w.astype(BF16), v)


def _sb_prompt_kernel(bias_ref, q_ref, k_ref, v_ref, uo_ref, o_ref, *, tile):
    h = pl.program_id(1)
    qi = pl.program_id(2)
    blk = SB_BLOCK
    nsub = tile // blk
    q = (q_ref[0] * (LOG2E * HEAD_DIM ** -0.5)).astype(BF16)
    bias = bias_ref[h] * LOG2E
    uo = uo_ref[...]
    visible = (lax.broadcasted_iota(jnp.int32, (blk, blk), 1)
               < lax.broadcasted_iota(jnp.int32, (blk, blk), 0))

    def keys(kt):
        rows = pl.ds(pl.multiple_of(kt * tile, tile), tile)
        return k_ref[0, rows, :].astype(BF16), v_ref[0, rows, :].astype(BF16)

    k, v = keys(qi)
    zeros = jnp.zeros((blk, blk), F32)
    parts = [_sb_rows(q[r * blk:(r + 1) * blk], k[:(r + 1) * blk], v[:(r + 1) * blk],
                      zeros, zeros, uo, bias, visible) for r in range(nsub)]
    carry = jnp.concatenate([p[0] for p in parts], axis=0)
    acc = jnp.concatenate([p[1] for p in parts], axis=0)

    def body(s, ca):
        k, v = keys(qi - 1 - s)
        return _sb_rows(q, k, v, ca[0], ca[1], uo, bias, None)

    carry, acc = lax.fori_loop(0, qi, body, (carry, acc))
    o_ref[...] = acc.astype(o_ref.dtype)


def _sb_prompt(proj, bias, batch, seq, tile):
    nq = seq // tile
    n = batch * seq
    uo = _strictly_later_const(min(tile, SB_SPAN))
    return pl.pallas_call(
        functools.partial(_sb_prompt_kernel, tile=tile),
        grid=(batch, N_HEADS, nq),
        in_specs=[
            pl.BlockSpec(memory_space=pltpu.SMEM),
            pl.BlockSpec((1, tile, HEAD_DIM), lambda b, h, i: (G_SQ, b * nq + i, h)),
            pl.BlockSpec((1, seq, HEAD_DIM), lambda b, h, i: (G_SK, b, h)),
            pl.BlockSpec((1, seq, HEAD_DIM), lambda b, h, i: (G_SV, b, h)),
            pl.BlockSpec(uo.shape, lambda b, h, i: (0, 0)),
        ],
        out_specs=pl.BlockSpec((tile, HEAD_DIM), lambda b, h, i: (b * nq + i, h)),
        out_shape=jax.ShapeDtypeStruct((n, D_MODEL), BF16),
        compiler_params=pltpu.CompilerParams(
            dimension_semantics=("parallel", "parallel", "arbitrary")),
        name="sb_prompt",
    )(bias, proj, proj, proj, uo)


def _sb_decode_kernel(pt_ref, q_ref, bias_ref, uo_ref, *refs, pages_per_step):
    k_refs = refs[:pages_per_step]
    v_refs = refs[pages_per_step:2 * pages_per_step]
    o_ref, carry_ref, acc_ref = refs[2 * pages_per_step:]
    s = pl.program_id(1)
    scale = HEAD_DIM ** -0.5
    blk = SB_BLOCK
    span = pages_per_step * blk
    head_of_row = lax.broadcasted_iota(jnp.int32, (N_HEADS, span), 0)

    @pl.when(s == 0)
    def _():
        carry_ref[...] = jnp.zeros_like(carry_ref)
        acc_ref[...] = jnp.zeros_like(acc_ref)

    def head_rows(refs, h):
        return jnp.concatenate(
            [ref[0, pl.ds(h, blk, stride=N_HEADS), :] for ref in refs], axis=0).astype(BF16)

    q = q_ref[0, 0].astype(BF16)
    z = jnp.zeros((N_HEADS, span), F32)
    for h in range(N_HEADS):
        z = jnp.where(head_of_row == h, _dot_nt(q, head_rows(k_refs, h)), z)
    z = z * scale + jnp.concatenate([bias_ref[...]] * pages_per_step, axis=1)
    log_b, log_keep = _log_beta_terms(z)
    lk_rows = jnp.concatenate([log_keep[:, r * blk:(r + 1) * blk] for r in range(pages_per_step)], axis=0)
    cs = _exact_dot_rhs01(lk_rows, uo_ref[...])
    carry = carry_ref[...]
    ws = []
    for r in range(pages_per_step):
        rows = slice(r * N_HEADS, (r + 1) * N_HEADS)
        ws.append(jnp.exp(log_b[:, r * blk:(r + 1) * blk] + cs[rows, :blk] + carry).astype(BF16))
        carry = carry + cs[rows, blk:]
    carry_ref[...] = carry
    w = jnp.concatenate(ws, axis=1)
    for h in range(N_HEADS):
        acc_ref[h] += _dot(w, head_rows(v_refs, h))

    @pl.when(s == pl.num_programs(1) - 1)
    def _():
        out = jnp.zeros((N_HEADS, HEAD_DIM), F32)
        for h in range(N_HEADS):
            out = jnp.where(head_of_row[:, :HEAD_DIM] == h, acc_ref[h], out)
        o_ref[0] = out


def _sb_decode(proj, bias_b, cache_k, cache_v, page_table, pages_per_step):
    nb, n_pages = page_table.shape
    n_pool, page = cache_k.shape[0], cache_k.shape[1]
    assert page == SB_BLOCK
    ck = cache_k.reshape(n_pool, page * N_HEADS, HEAD_DIM)
    cv = cache_v.reshape(n_pool, page * N_HEADS, HEAD_DIM)
    proj4 = proj.reshape(N_GROUPS, nb, N_HEADS, HEAD_DIM)
    uo = _later_sum_consts()
    pt = page_table.reshape(-1)

    def page_map(r):
        return lambda b, s, pt: (pt[b * n_pages + n_pages - 1 - (s * pages_per_step + r)], 0, 0)

    def page_specs():
        return [pl.BlockSpec((1, page * N_HEADS, HEAD_DIM), page_map(r)) for r in range(pages_per_step)]

    grid_spec = pltpu.PrefetchScalarGridSpec(
        num_scalar_prefetch=1,
        grid=(nb, n_pages // pages_per_step),
        in_specs=[
            pl.BlockSpec((1, 1, N_HEADS, HEAD_DIM), lambda b, s, pt: (G_SQ, b, 0, 0)),
            pl.BlockSpec(bias_b.shape, lambda b, s, pt: (0, 0)),
            pl.BlockSpec(uo.shape, lambda b, s, pt: (0, 0)),
        ] + page_specs() + page_specs(),
        out_specs=pl.BlockSpec((1, N_HEADS, HEAD_DIM), lambda b, s, pt: (b, 0, 0)),
        scratch_shapes=[pltpu.VMEM((N_HEADS, SB_BLOCK), F32),
                        pltpu.VMEM((N_HEADS, N_HEADS, HEAD_DIM), F32)],
    )
    out = pl.pallas_call(
        functools.partial(_sb_decode_kernel, pages_per_step=pages_per_step),
        grid_spec=grid_spec,
        out_shape=jax.ShapeDtypeStruct((nb, N_HEADS, HEAD_DIM), F32),
        compiler_params=pltpu.CompilerParams(dimension_semantics=("parallel", "arbitrary")),
        name="sb_decode",
    )(pt, proj4, bias_b, uo, *([ck] * pages_per_step), *([cv] * pages_per_step))
    return out.reshape(nb, D_MODEL)


def _merge_kernel(oa_ref, ob_ref, ga_ref, gb_ref, x_ref, wa_ref, wb_ref, wo_ref, gf_ref,
                  x1_ref, h2_ref):
    a = _dot(oa_ref[...].astype(BF16), wa_ref[...])
    b = _dot(ob_ref[...].astype(BF16), wb_ref[...])
    m = jax.nn.sigmoid(ga_ref[0]) * a + jax.nn.sigmoid(gb_ref[0]) * b
    x1 = x_ref[...] + _dot(m.astype(BF16), wo_ref[...])
    x1_ref[...] = x1
    h2_ref[...] = _rms(x1, gf_ref[...]).astype(BF16)


def _merge(o_a, o_b, proj, x, wa, wb, wo, g_ffn, tm):
    n = x.shape[0]
    row = pl.BlockSpec((tm, D_MODEL), lambda i: (i, 0))
    wspec = pl.BlockSpec((D_MODEL, D_MODEL), lambda i: (0, 0))
    return pl.pallas_call(
        _merge_kernel,
        grid=(n // tm,),
        in_specs=[
            row, row,
            pl.BlockSpec((1, tm, D_MODEL), lambda i: (G_GA, i, 0)),
            pl.BlockSpec((1, tm, D_MODEL), lambda i: (G_GB, i, 0)),
            row, wspec, wspec, wspec,
            pl.BlockSpec((1, D_MODEL), lambda i: (0, 0)),
        ],
        out_specs=[row, row],
        out_shape=[jax.ShapeDtypeStruct((n, D_MODEL), F32), jax.ShapeDtypeStruct((n, D_MODEL), BF16)],
        compiler_params=pltpu.CompilerParams(dimension_semantics=("parallel",)),
        name="merge",
    )(o_a, o_b, proj, proj, x, wa, wb, wo, g_ffn)


def _top_rows(vals, order, n):
    out_v, out_o = [], []
    for _ in range(n):
        m = jnp.max(vals, axis=0, keepdims=True)
        first = jnp.min(jnp.where(vals == m, order, 1e9), axis=0, keepdims=True)
        out_v.append(m)
        out_o.append(first)
        vals = jnp.where(order == first, NEG_INF, vals)
    return out_v, out_o


def _stack_rows(rows, sub):
    out = jnp.zeros(sub.shape, F32)
    for r, row in enumerate(rows):
        out = jnp.where(sub == r, row, out)
    return out


def _pair_candidates():
    pairs = [(0, b) for b in range(16)] + [(1, b) for b in range(8)]
    pairs += [(2, b) for b in range(5)] + [(4, b) for b in range(3)]
    pairs += [(3, b) for b in range(4)] + [(5, 0), (5, 1), (6, 0), (6, 1)]
    pairs += [(7, 0), (7, 1)] + [(a, 0) for a in range(8, 16)]
    assert sorted(pairs) == sorted((a, b) for a in range(16) for b in range(16) if (a + 1) * (b + 1) <= 16)
    return pairs + [(-1, 0)] * (-len(pairs) % 8)


def _peer_route_kernel(h_ref, wq_ref, sk_ref, order_ref, i1_ref, i2_ref, gate_ref):
    t = h_ref.shape[0]
    q = _dot(h_ref[...], wq_ref[...]).astype(BF16)
    key_iota = lax.broadcasted_iota(jnp.int32, (PEER_KEYS, t), 0).astype(F32)
    sub16 = lax.broadcasted_iota(jnp.int32, (PEER_TOPK, t), 0)
    sub8 = lax.broadcasted_iota(jnp.int32, (8, t), 0)
    pairs = _pair_candidates()
    order = order_ref[...]
    k1_rows, k2_rows, g_rows = [], [], []
    for h in range(N_HEADS):
        top_v, top_i = [], []
        for p in range(2):
            c0 = (h * 2 + p) * PEER_KEYS
            s = _dot_nt(sk_ref[p], q[:, c0:c0 + PEER_KEYS])
            v, i = _top_rows(s, key_iota, PEER_TOPK)
            top_v.append(v)
            top_i.append(i)
        cand = []
        for r0 in range(0, len(pairs), 8):
            tile = pairs[r0:r0 + 8]
            va = _stack_rows([top_v[0][max(a, 0)] for a, _ in tile], sub8)
            vb = _stack_rows([top_v[1][b] for _, b in tile], sub8)
            cand.append(va + vb)
        cand = jnp.where(order < 1e6, jnp.concatenate(cand, axis=0), NEG_INF)
        best, flat = _top_rows(cand, order, PEER_TOPK)
        flat = _stack_rows(flat, sub16)
        rank_a = jnp.floor(flat * (1.0 / PEER_TOPK))
        rank_b = flat - rank_a * float(PEER_TOPK)
        key1 = jnp.zeros_like(flat)
        key2 = jnp.zeros_like(flat)
        for r in range(PEER_TOPK):
            key1 = jnp.where(rank_a == float(r), top_i[0][r], key1)
            key2 = jnp.where(rank_b == float(r), top_i[1][r], key2)
        top = best[0]
        best = _stack_rows(best, sub16)
        ex = jnp.exp(best - top)
        g_rows.append(ex / jnp.sum(ex, axis=0, keepdims=True))
        k1_rows.append(key1)
        k2_rows.append(key2)
    i1_ref[...] = jnp.concatenate(k1_rows, axis=0).T
    i2_ref[...] = jnp.concatenate(k2_rows, axis=0).T
    gate_ref[...] = jnp.concatenate(g_rows, axis=0).T


def _peer_route(h2, wq, sub_keys, tt):
    n = h2.shape[0]
    sel = pl.BlockSpec((tt, PEER_SEL), lambda i: (i, 0))
    shp = jax.ShapeDtypeStruct((n, PEER_SEL), F32)
    flat = np.array([a * PEER_TOPK + b if a >= 0 else 1e9 for a, b in _pair_candidates()], np.float32)
    order = jnp.asarray(np.broadcast_to(flat[:, None], (flat.shape[0], tt)))
    return pl.pallas_call(
        _peer_route_kernel,
        grid=(n // tt,),
        in_specs=[
            pl.BlockSpec((tt, D_MODEL), lambda i: (i, 0)),
            pl.BlockSpec(wq.shape, lambda i: (0, 0)),
            pl.BlockSpec(sub_keys.shape, lambda i: (0, 0, 0)),
            pl.BlockSpec(order.shape, lambda i: (0, 0)),
        ],
        out_specs=[sel, sel, sel],
        out_shape=[shp, shp, shp],
        compiler_params=pltpu.CompilerParams(dimension_semantics=("parallel",)),
        name="peer_route",
    )(h2, wq, sub_keys, order)


def _slab_pitch(tt):
    p = tt // SUBLANES
    return SUBLANES * (p + 1 if p % 2 == 0 else p + 2)


def _peer_expert_kernel(h_ref, ut_ref, v_ref, i1_ref, i2_ref, gate_ref, x_ref, gf_ref,
                        y_ref, slab_ref, *, n_up, n_down, pitch, final_norm):
    s = pl.program_id(1)
    tt = h_ref.shape[0]
    slabs_up = ut_ref.shape[1] // PEER_KEYS
    slabs_down = v_ref.shape[0] // PEER_KEYS
    key_iota = lax.broadcasted_iota(jnp.int32, (PEER_KEYS, PEER_SEL), 0).astype(F32)

    def slab_rows(g):
        return pl.ds(pl.multiple_of(g * pitch, 8), tt)

    def token_rows(t):
        return pl.ds(t, PEER_KEYS, stride=pitch)

    @pl.when(s < n_up)
    def _():
        a = _dot(h_ref[...], ut_ref[...])
        for gl in range(slabs_up):
            slab_ref[slab_rows(s * slabs_up + gl), :] = a[:, gl * PEER_KEYS:(gl + 1) * PEER_KEYS]

    @pl.when(s == n_up - 1)
    def _():
        sub = lax.broadcasted_iota(jnp.int32, (PEER_GROUP, PEER_SEL), 0)

        def group_base(gi):
            base = gi * PEER_GROUP
            return base if isinstance(base, int) else pl.multiple_of(base, PEER_GROUP)

        def hits(t):
            return (key_iota == i1_ref[pl.ds(t, 1), :],
                    (key_iota == i2_ref[pl.ds(t, 1), :]).astype(BF16))

        def select(gi):
            base = group_base(gi)
            rows = []
            for u in range(PEER_GROUP):
                t = base + u
                hit1, hit2 = hits(t)
                pre = slab_ref[token_rows(t), :]
                col = jnp.broadcast_to(i2_ref[pl.ds(t, 1), :].astype(jnp.int32), (PEER_KEYS, PEER_SEL))
                picked = jnp.take_along_axis(pre, col, axis=1)
                rows.append(jnp.sum(jnp.where(hit1, picked, 0.0), axis=0, keepdims=True))
            return _stack_rows(rows, sub)

        def scatter(gi, pre):
            base = group_base(gi)
            wv = gate_ref[pl.ds(base, PEER_GROUP), :] * (0.5 * pre * (1.0 + lax.erf(pre * (0.5 ** 0.5))))
            for u in range(PEER_GROUP):
                hit1, hit2 = hits(base + u)
                w1 = jnp.where(hit1, wv[u:u + 1, :], 0.0).astype(BF16)
                slab_ref[token_rows(base + u), :] = _dot_nt(w1, hit2)

        def step(gi, pre_prev):
            pre = select(gi)
            scatter(gi - 1, pre_prev)
            return pre

        n_groups = tt // PEER_GROUP
        pre_last = lax.fori_loop(1, n_groups, step, select(0), unroll=PEER_GROUP_UNROLL)
        scatter(n_groups - 1, pre_last)
        y_ref[...] = jnp.zeros_like(y_ref)

    @pl.when(s >= n_up)
    def _():
        c = s - n_up
        w = jnp.concatenate(
            [slab_ref[slab_rows(c * slabs_down + gl), :].astype(BF16) for gl in range(slabs_down)], axis=1)
        y_ref[...] += _dot(w, v_ref[...])

    @pl.when(s == n_up + n_down - 1)
    def _():
        x2 = x_ref[...] + y_ref[...]
        y_ref[...] = _rms(x2, gf_ref[...]) if final_norm else x2


def _peer_experts(h2, u_t, v_tab, i1, i2, gates, x1, g_final, final_norm, tt, chunk_up, chunk_down):
    n = h2.shape[0]
    n_exp = v_tab.shape[0]
    n_up, n_down = n_exp // chunk_up, n_exp // chunk_down
    pitch = _slab_pitch(tt)
    row = pl.BlockSpec((tt, D_MODEL), lambda i, s: (i, 0))
    sel = pl.BlockSpec((tt, PEER_SEL), lambda i, s: (i, 0))
    return pl.pallas_call(
        functools.partial(_peer_expert_kernel, n_up=n_up, n_down=n_down, pitch=pitch, final_norm=final_norm),
        grid=(n // tt, n_up + n_down),
        in_specs=[
            row,
            pl.BlockSpec((D_MODEL, chunk_up), lambda i, s: (0, jnp.minimum(s, n_up - 1))),
            pl.BlockSpec((chunk_down, D_MODEL), lambda i, s: (jnp.maximum(s - n_up, 0), 0)),
            sel, sel, sel, row,
            pl.BlockSpec((1, D_MODEL), lambda i, s: (0, 0)),
        ],
        out_specs=row,
        out_shape=jax.ShapeDtypeStruct((n, D_MODEL), F32),
        scratch_shapes=[pltpu.VMEM((PEER_KEYS * pitch, PEER_KEYS), F32)],
        compiler_params=pltpu.CompilerParams(dimension_semantics=("parallel", "arbitrary")),
        name="peer_experts",
    )(h2, u_t, v_tab, i1, i2, gates, x1, g_final)


def _tile(n, target):
    return min(n, target)


def kernel(x_prompt, x_sample, cache_k, cache_v, state_hgrn, page_table, g_attn, w_in, hg_lb_logits,
           hg_norm, sb_bias, w_branch_a, w_branch_b, w_out, g_ffn, peer_wq, peer_subkeys, peer_u, peer_v,
           g_final):
    depth = w_in.shape[0]
    batch, seq, _ = x_prompt.shape
    nb, dec_seq, _ = x_sample.shape
    assert dec_seq == 1
    n_p = batch * seq
    xp = x_prompt.reshape(n_p, D_MODEL)
    xs = x_sample.reshape(nb, D_MODEL)
    g_fin = g_final.reshape(1, D_MODEL)
    outs = {k: [] for k in ("sp", "kp", "vp", "ss", "ks", "vs")}
    for l in range(depth):
        w_in_l = w_in[l].astype(BF16)
        wa, wb, wo = (w[l].astype(BF16) for w in (w_branch_a, w_branch_b, w_out))
        wq = peer_wq[l].astype(BF16)
        sk = peer_subkeys[l].astype(BF16)
        u_t = peer_u[l].astype(BF16).T
        v_tab = peer_v[l].astype(BF16)
        g_a = g_attn[l].reshape(1, D_MODEL)
        g_f = g_ffn[l].reshape(1, D_MODEL)
        ng = hg_norm[l].reshape(1, D_MODEL)
        bias_b = jnp.broadcast_to(sb_bias[l].astype(F32)[:, None], (N_HEADS, SB_BLOCK))
        last = l == depth - 1

        proj, k_new, v_new = _norm_proj(xp, g_a, w_in_l, _tile(n_p, PROJ_ROWS))
        o_a, s_p = _hgrn_prompt(proj, hg_lb_logits, ng, batch, seq, l, _tile(seq, HG_STEP_TOKENS))
        o_b = _sb_prompt(proj, sb_bias[l].astype(F32), batch, seq, _tile(seq, SB_TILE))
        x1, h2 = _merge(o_a, o_b, proj, xp, wa, wb, wo, g_f, _tile(n_p, MERGE_ROWS))
        i1, i2, gates = _peer_route(h2, wq, sk, _tile(n_p, ROUTE_TOKENS))
        xp = _peer_experts(h2, u_t, v_tab, i1, i2, gates, x1, g_fin, last, _tile(n_p, EXPERT_TOKENS),
                           EXPERT_CHUNK_UP, EXPERT_CHUNK_DOWN)
        outs["sp"].append(s_p)
        outs["kp"].append(k_new.reshape(batch, seq, N_HEADS, HEAD_DIM))
        outs["vp"].append(v_new.reshape(batch, seq, N_HEADS, HEAD_DIM))

        proj_s, k_new_s, v_new_s = _norm_proj(xs, g_a, w_in_l, nb)
        o_a_s, s_s = _hgrn_step(proj_s, hg_lb_logits, hg_norm[l], state_hgrn[l], l)
        o_b_s = _sb_decode(proj_s, bias_b, cache_k[l], cache_v[l], page_table, DECODE_PAGES)
        x1_s, h2_s = _merge(o_a_s.reshape(nb, D_MODEL), o_b_s, proj_s, xs, wa, wb, wo, g_f, nb)
        pad = (-nb) % PEER_KEYS
        sel_s = _peer_route(jnp.pad(h2_s, ((0, pad), (0, 0))), wq, sk, PEER_KEYS)
        i1_s, i2_s, gates_s = (a[:nb] for a in sel_s)
        xs = _peer_experts(h2_s, u_t, v_tab, i1_s, i2_s, gates_s, x1_s, g_fin, last, nb,
                           EXPERT_CHUNK_DOWN, EXPERT_CHUNK_DOWN)
        outs["ss"].append(s_s)
        outs["ks"].append(k_new_s.reshape(nb, 1, N_HEADS, HEAD_DIM))
        outs["vs"].append(v_new_s.reshape(nb, 1, N_HEADS, HEAD_DIM))

    y_prompt = xp.reshape(batch, seq, D_MODEL)
    y_sample = xs.reshape(nb, 1, D_MODEL)
    return (y_prompt, y_sample, jnp.stack(outs["sp"]), jnp.stack(outs["kp"]), jnp.stack(outs["vp"]),
            jnp.stack(outs["ss"]), jnp.stack(outs["ks"]), jnp.stack(outs["vs"]))
```

```python
import functools

import numpy as np
import jax
import jax.numpy as jnp
from jax import lax
from jax.experimental import pallas as pl
from jax.experimental.pallas import tpu as pltpu

F32 = jnp.float32
BF16 = jnp.bfloat16

D_MODEL = 1024
N_HEADS = 8
HEAD_DIM = 128
N_GROUPS = 9
SUBLANES = 8
V7X_VMEM_BYTES = 64 * 1024 * 1024
VMEM_HEADROOM_BYTES = 2 * 1024 * 1024
HG_CHUNK = 128
HG_STEP_TOKENS = 2048
PROJ_ROWS = 1024
MERGE_ROWS = 512
ROUTE_TOKENS = 256
EXPERT_TOKENS = 512
EXPERT_CHUNK_UP = 2048
EXPERT_CHUNK_DOWN = 2048
DECODE_PAGES = 16
SB_BLOCK = 128
SB_TILE = 1024
SB_SPAN = 256
PEER_GROUP = 8
PEER_GROUP_UNROLL = 2
PEER_KEYS = 128
PEER_TOPK = 16
PEER_SEL = N_HEADS * PEER_TOPK
RMS_EPS = 1e-6
NEG_INF = float("-inf")
LOG2E = 1.4426950408889634

G_HQ, G_HF, G_HI, G_HG, G_SQ, G_SK, G_SV, G_GA, G_GB = range(N_GROUPS)


def _dot(a, b):
    return jnp.dot(a, b, preferred_element_type=F32)


def _dot_nt(a, b):
    return lax.dot_general(a, b, (((1,), (1,)), ((), ())), preferred_element_type=F32)


def _dot_tn(a, b):
    return lax.dot_general(a, b, (((0,), (0,)), ((), ())), preferred_element_type=F32)


def _split3(x):
    hi = x.astype(BF16)
    r = x - hi.astype(F32)
    mid = r.astype(BF16)
    lo = (r - mid.astype(F32)).astype(BF16)
    return hi, mid, lo


def _exact_dot_lhs01(m01, x):
    n = x.shape[1]
    parts = _dot(m01, jnp.concatenate(_split3(x), axis=1))
    return parts[:, :n] + parts[:, n:2 * n] + parts[:, 2 * n:]


def _exact_dot_rhs01(x, m01):
    hi, mid, lo = _split3(x)
    return _dot(hi, m01) + _dot(mid, m01) + _dot(lo, m01)


def _rms(x, g):
    return x * lax.rsqrt(jnp.mean(x * x, axis=-1, keepdims=True) + RMS_EPS) * g


def _silu(x):
    return x * jax.nn.sigmoid(x)


def _norm_proj_kernel(x_ref, g_ref, w_ref, o_ref, k_ref, v_ref, h_ref):
    j = pl.program_id(1)
    tm = x_ref.shape[0]

    @pl.when(j == 0)
    def _():
        h_ref[...] = _rms(x_ref[...], g_ref[...]).astype(BF16)

    out = _dot(h_ref[...], w_ref[...])
    o_ref[0] = out

    for group, ref in ((G_SK, k_ref), (G_SV, v_ref)):
        @pl.when(j == group)
        def _():
            for h in range(N_HEADS):
                ref[pl.ds(h, tm, stride=N_HEADS), :] = out[:, h * HEAD_DIM:(h + 1) * HEAD_DIM]


def _norm_proj(x, g, w, tm):
    n = x.shape[0]
    kv_spec = pl.BlockSpec((tm * N_HEADS, HEAD_DIM), lambda i, j: (i, 0))
    kv_shape = jax.ShapeDtypeStruct((n * N_HEADS, HEAD_DIM), F32)
    return pl.pallas_call(
        _norm_proj_kernel,
        grid=(n // tm, N_GROUPS),
        in_specs=[
            pl.BlockSpec((tm, D_MODEL), lambda i, j: (i, 0)),
            pl.BlockSpec((1, D_MODEL), lambda i, j: (0, 0)),
            pl.BlockSpec((D_MODEL, D_MODEL), lambda i, j: (0, j)),
        ],
        out_specs=[pl.BlockSpec((1, tm, D_MODEL), lambda i, j: (j, i, 0)), kv_spec, kv_spec],
        out_shape=[jax.ShapeDtypeStruct((N_GROUPS, n, D_MODEL), F32), kv_shape, kv_shape],
        scratch_shapes=[pltpu.VMEM((tm, D_MODEL), BF16)],
        compiler_params=pltpu.CompilerParams(dimension_semantics=("parallel", "arbitrary")),
        name="norm_proj",
    )(x, g, w)


def _hgrn_consts():
    c = HG_CHUNK
    t = np.arange(c)[:, None]
    j = np.arange(c)[None, :]
    sel = [(j <= t)]
    masks = []
    m = c // 2
    while m >= 1:
        mid = (t // (2 * m)) * (2 * m) + m
        upper = t >= mid
        if m < SUBLANES:
            sel.append(np.where(upper, (j >= mid) & (j <= t), (j > t) & (j <= mid - 1)))
        s = j
        same = (t // (2 * m)) == (s // (2 * m))
        s_mid = (s // (2 * m)) * (2 * m) + m
        masks.append(same & upper & (s < s_mid))
        m //= 2
    masks.append(t == j)
    sel = np.concatenate(sel, axis=0).astype(np.float32)
    masks = np.stack(masks).astype(np.float32)
    return jnp.asarray(sel, dtype=BF16), jnp.asarray(masks, dtype=F32)


def _lower_bound(lbl, layer):
    e = jnp.exp(lbl - jnp.max(lbl, axis=0, keepdims=True))
    return jnp.sum(e[: layer + 1], axis=0, keepdims=True) / jnp.sum(e, axis=0, keepdims=True)


def _level_exponent(cum, m):
    parts = []
    for b0 in range(0, cum.shape[0], 2 * m):
        ref = cum[b0 + m - 1:b0 + m]
        parts += [ref - cum[b0:b0 + m], cum[b0 + m:b0 + 2 * m] - ref]
    return jnp.concatenate(parts, axis=0)


def _hgrn_kernel(q_ref, f_ref, i_ref, g_ref, lbl_ref, ng_ref, sel_ref, mask_ref,
                 o_ref, s_ref, st_ref, *, n_sub, layer):
    c = pl.program_id(2)
    n_lvl = mask_ref.shape[0] - 1
    ch = HG_CHUNK
    halves = [ch >> (li + 1) for li in range(n_lvl)]

    @pl.when(c == 0)
    def _():
        st_ref[...] = jnp.zeros_like(st_ref)

    lb = _lower_bound(lbl_ref[...], layer)
    ng = ng_ref[...]
    sel = sel_ref[...]

    chunks = []
    for sc in range(n_sub):
        rows = slice(sc * ch, (sc + 1) * ch)
        f = lb + (1.0 - lb) * jax.nn.sigmoid(f_ref[0, rows, :])
        logf = jnp.log(f)
        k = 1.0 - f
        q = _silu(q_ref[0, rows, :])
        v = i_ref[0, rows, :].astype(BF16)
        sums = _exact_dot_lhs01(sel, logf)
        cum = sums[0:ch]
        scores = _dot_nt(q.astype(BF16), k.astype(BF16)) * mask_ref[n_lvl]
        small = 0
        for li, m in enumerate(halves):
            if m >= SUBLANES:
                el = jnp.exp(_level_exponent(cum, m))
            else:
                small += 1
                el = jnp.exp(sums[small * ch:(small + 1) * ch])
            scores += _dot_nt((q * el).astype(BF16), (k * el).astype(BF16)) * mask_ref[li]
        e_cum = jnp.exp(cum)
        e_last = jnp.exp(cum[ch - 1:ch] - cum)
        chunks.append((
            (q * e_cum).astype(BF16),
            _dot(scores.astype(BF16), v),
            e_cum[ch - 1:ch],
            _dot_tn(v, (k * e_last).astype(BF16)),
        ))

    st = st_ref[...]
    for sc, (q_dec, o_intra, decay, kv) in enumerate(chunks):
        rows = slice(sc * ch, (sc + 1) * ch)
        o = o_intra + _dot_nt(q_dec, st.astype(BF16))
        st = st * decay + kv
        o = o * lax.rsqrt(jnp.mean(o * o, axis=-1, keepdims=True) + RMS_EPS)
        o_ref[rows, :] = (o * ng * _silu(g_ref[0, rows, :])).astype(o_ref.dtype)
    st_ref[...] = st

    @pl.when(c == pl.num_programs(2) - 1)
    def _():
        s_ref[0, 0] = st_ref[...].T


def _hgrn_prompt(proj, lb_logits, norm_g, batch, seq, layer, tc):
    sel, masks = _hgrn_consts()
    nc = seq // tc
    n = batch * seq

    def pmap(g):
        return lambda b, h, c: (g, b * nc + c, h)

    kern = functools.partial(_hgrn_kernel, n_sub=tc // HG_CHUNK, layer=layer)
    return pl.pallas_call(
        kern,
        grid=(batch, N_HEADS, nc),
        in_specs=[
            pl.BlockSpec((1, tc, HEAD_DIM), pmap(G_HQ)),
            pl.BlockSpec((1, tc, HEAD_DIM), pmap(G_HF)),
            pl.BlockSpec((1, tc, HEAD_DIM), pmap(G_HI)),
            pl.BlockSpec((1, tc, HEAD_DIM), pmap(G_HG)),
            pl.BlockSpec((lb_logits.shape[0], HEAD_DIM), lambda b, h, c: (0, h)),
            pl.BlockSpec((1, HEAD_DIM), lambda b, h, c: (0, h)),
            pl.BlockSpec(sel.shape, lambda b, h, c: (0, 0)),
            pl.BlockSpec(masks.shape, lambda b, h, c: (0, 0, 0)),
        ],
        out_specs=[
            pl.BlockSpec((tc, HEAD_DIM), lambda b, h, c: (b * nc + c, h)),
            pl.BlockSpec((1, 1, HEAD_DIM, HEAD_DIM), lambda b, h, c: (b, h, 0, 0)),
        ],
        out_shape=[
            jax.ShapeDtypeStruct((n, D_MODEL), BF16),
            jax.ShapeDtypeStruct((batch, N_HEADS, HEAD_DIM, HEAD_DIM), F32),
        ],
        scratch_shapes=[pltpu.VMEM((HEAD_DIM, HEAD_DIM), F32)],
        compiler_params=pltpu.CompilerParams(
            dimension_semantics=("parallel", "parallel", "arbitrary")),
        name="hgrn_prompt",
    )(proj, proj, proj, proj, lb_logits, norm_g, sel, masks)


def _hgrn_step_kernel(q_ref, f_ref, i_ref, g_ref, lbl_ref, ng_ref, s_ref, o_ref, sn_ref, *, layer):
    lb = _lower_bound(lbl_ref[...], layer)
    lb = lb[0]
    f = lb + (1.0 - lb) * jax.nn.sigmoid(f_ref[0, 0])
    q = _silu(q_ref[0, 0])
    v = i_ref[0, 0]
    g = g_ref[0, 0]
    ng = ng_ref[...]
    pad = jnp.zeros((HEAD_DIM - 2 * N_HEADS, HEAD_DIM), F32)
    cols = jnp.concatenate([f, q, pad], axis=0).T
    for h in range(N_HEADS):
        fcol = cols[:, h:h + 1]
        qcol = cols[:, N_HEADS + h:N_HEADS + h + 1]
        s_new = fcol * s_ref[0, h] + (1.0 - fcol) * v[h:h + 1, :]
        sn_ref[0, h] = s_new
        o = jnp.sum(qcol * s_new, axis=0, keepdims=True)
        o = o * lax.rsqrt(jnp.mean(o * o, axis=-1, keepdims=True) + RMS_EPS)
        o_ref[0, pl.ds(h, 1), :] = o * ng[h:h + 1, :] * _silu(g[h:h + 1, :])


def _hgrn_step(proj, lb_logits, norm_g, state, layer):
    nb = state.shape[0]
    proj4 = proj.reshape(N_GROUPS, nb, N_HEADS, HEAD_DIM)
    lbl = lb_logits.reshape(lb_logits.shape[0], N_HEADS, HEAD_DIM)
    ng = norm_g.reshape(N_HEADS, HEAD_DIM)

    def pmap(g):
        return lambda b: (g, b, 0, 0)

    blk = (1, 1, N_HEADS, HEAD_DIM)
    return pl.pallas_call(
        functools.partial(_hgrn_step_kernel, layer=layer),
        grid=(nb,),
        in_specs=[
            pl.BlockSpec(blk, pmap(G_HQ)),
            pl.BlockSpec(blk, pmap(G_HF)),
            pl.BlockSpec(blk, pmap(G_HI)),
            pl.BlockSpec(blk, pmap(G_HG)),
            pl.BlockSpec(lbl.shape, lambda b: (0, 0, 0)),
            pl.BlockSpec(ng.shape, lambda b: (0, 0)),
            pl.BlockSpec((1, N_HEADS, HEAD_DIM, HEAD_DIM), lambda b: (b, 0, 0, 0)),
        ],
        out_specs=[
            pl.BlockSpec((1, N_HEADS, HEAD_DIM), lambda b: (b, 0, 0)),
            pl.BlockSpec((1, N_HEADS, HEAD_DIM, HEAD_DIM), lambda b: (b, 0, 0, 0)),
        ],
        out_shape=[
            jax.ShapeDtypeStruct((nb, N_HEADS, HEAD_DIM), F32),
            jax.ShapeDtypeStruct(state.shape, F32),
        ],
        name="hgrn_step",
    )(proj4, proj4, proj4, proj4, lbl, ng, state)


def _later_sum_consts():
    j = np.arange(SB_BLOCK)[:, None]
    k = np.arange(SB_BLOCK)[None, :]
    u = np.concatenate([(j > k), np.ones((SB_BLOCK, SB_BLOCK), bool)], axis=1)
    return jnp.asarray(u.astype(np.float32), dtype=BF16)


def _log_beta_terms(z):
    l1p = jnp.log(1.0 + jnp.exp(-jnp.abs(z)))
    return jnp.minimum(z, 0.0) - l1p, -jnp.maximum(z, 0.0) - l1p


def _strictly_later_const(span):
    j = np.arange(span)[:, None]
    k = np.arange(span)[None, :]
    return jnp.asarray((j > k).astype(np.float32), dtype=BF16)


def _mask_last_block(x, visible):
    cut = x.shape[1] - SB_BLOCK
    last = jnp.where(visible, x[:, cut:], 0.0)
    return last if cut == 0 else jnp.concatenate([x[:, :cut], last], axis=1)


def _sb_rows(q, k, v, carry, acc, later01, bias, visible):
    blk = SB_BLOCK
    span = later01.shape[0]
    n_keys = k.shape[0]
    z = _dot_nt(q, k) + bias
    neg_abs = lax.bitcast_convert_type(lax.bitcast_convert_type(z, jnp.uint32) | jnp.uint32(0x80000000), F32)
    log_b = jnp.minimum(z, 0.0) - jnp.log2(1.0 + jnp.exp2(neg_abs))
    log_keep = log_b - z
    if visible is not None:
        log_keep = _mask_last_block(log_keep, visible)
    ws = []
    for c0 in reversed(range(0, n_keys, span)):
        wd = min(span, n_keys - c0)
        lk = log_keep[:, c0:c0 + wd]
        u = later01 if wd == span else later01[:wd, :wd]
        later = _dot(lk.astype(BF16), u)
        ws.insert(0, jnp.exp2(log_b[:, c0:c0 + wd] + later + jnp.concatenate([carry] * (wd // blk), axis=1)))
        carry = carry + jnp.sum(lk, axis=1, keepdims=True)
    w = jnp.concatenate(ws, axis=1)
    if visible is not None:
        w = _mask_last_block(w, visible)
    return carry, acc + _dot(w.astype(BF16), v)


def _sb_prompt_kernel(bias_ref, q_ref, k_ref, v_ref, uo_ref, o_ref, *, tile):
    h = pl.program_id(1)
    qi = pl.program_id(2)
    blk = SB_BLOCK
    nsub = tile // blk
    q = (q_ref[0] * (LOG2E * HEAD_DIM ** -0.5)).astype(BF16)
    bias = bias_ref[h] * LOG2E
    uo = uo_ref[...]
    visible = (lax.broadcasted_iota(jnp.int32, (blk, blk), 1)
               < lax.broadcasted_iota(jnp.int32, (blk, blk), 0))

    def keys(kt):
        rows = pl.ds(pl.multiple_of(kt * tile, tile), tile)
        return k_ref[0, rows, :].astype(BF16), v_ref[0, rows, :].astype(BF16)

    k, v = keys(qi)
    zeros = jnp.zeros((blk, blk), F32)
    parts = [_sb_rows(q[r * blk:(r + 1) * blk], k[:(r + 1) * blk], v[:(r + 1) * blk],
                      zeros, zeros, uo, bias, visible) for r in range(nsub)]
    carry = jnp.concatenate([p[0] for p in parts], axis=0)
    acc = jnp.concatenate([p[1] for p in parts], axis=0)

    def body(s, ca):
        k, v = keys(qi - 1 - s)
        return _sb_rows(q, k, v, ca[0], ca[1], uo, bias, None)

    carry, acc = lax.fori_loop(0, qi, body, (carry, acc))
    o_ref[...] = acc.astype(o_ref.dtype)


def _sb_prompt(proj, bias, batch, seq, tile):
    nq = seq // tile
    n = batch * seq
    uo = _strictly_later_const(min(tile, SB_SPAN))
    return pl.pallas_call(
        functools.partial(_sb_prompt_kernel, tile=tile),
        grid=(batch, N_HEADS, nq),
        in_specs=[
            pl.BlockSpec(memory_space=pltpu.SMEM),
            pl.BlockSpec((1, tile, HEAD_DIM), lambda b, h, i: (G_SQ, b * nq + i, h)),
            pl.BlockSpec((1, seq, HEAD_DIM), lambda b, h, i: (G_SK, b, h)),
            pl.BlockSpec((1, seq, HEAD_DIM), lambda b, h, i: (G_SV, b, h)),
            pl.BlockSpec(uo.shape, lambda b, h, i: (0, 0)),
        ],
        out_specs=pl.BlockSpec((tile, HEAD_DIM), lambda b, h, i: (b * nq + i, h)),
        out_shape=jax.ShapeDtypeStruct((n, D_MODEL), BF16),
        compiler_params=pltpu.CompilerParams(
            dimension_semantics=("parallel", "parallel", "arbitrary")),
        name="sb_prompt",
    )(bias, proj, proj, proj, uo)


def _sb_decode_kernel(pt_ref, q_ref, bias_ref, uo_ref, *refs, pages_per_step):
    k_refs = refs[:pages_per_step]
    v_refs = refs[pages_per_step:2 * pages_per_step]
    o_ref, carry_ref, acc_ref = refs[2 * pages_per_step:]
    s = pl.program_id(1)
    scale = HEAD_DIM ** -0.5
    blk = SB_BLOCK
    span = pages_per_step * blk
    head_of_row = lax.broadcasted_iota(jnp.int32, (N_HEADS, span), 0)

    @pl.when(s == 0)
    def _():
        carry_ref[...] = jnp.zeros_like(carry_ref)
        acc_ref[...] = jnp.zeros_like(acc_ref)

    def head_rows(refs, h):
        return jnp.concatenate(
            [ref[0, pl.ds(h, blk, stride=N_HEADS), :] for ref in refs], axis=0).astype(BF16)

    q = q_ref[0, 0].astype(BF16)
    z = jnp.zeros((N_HEADS, span), F32)
    for h in range(N_HEADS):
        z = jnp.where(head_of_row == h, _dot_nt(q, head_rows(k_refs, h)), z)
    z = z * scale + jnp.concatenate([bias_ref[...]] * pages_per_step, axis=1)
    log_b, log_keep = _log_beta_terms(z)
    lk_rows = jnp.concatenate([log_keep[:, r * blk:(r + 1) * blk] for r in range(pages_per_step)], axis=0)
    cs = _exact_dot_rhs01(lk_rows, uo_ref[...])
    carry = carry_ref[...]
    ws = []
    for r in range(pages_per_step):
        rows = slice(r * N_HEADS, (r + 1) * N_HEADS)
        ws.append(jnp.exp(log_b[:, r * blk:(r + 1) * blk] + cs[rows, :blk] + carry).astype(BF16))
        carry = carry + cs[rows, blk:]
    carry_ref[...] = carry
    w = jnp.concatenate(ws, axis=1)
    for h in range(N_HEADS):
        acc_ref[h] += _dot(w, head_rows(v_refs, h))

    @pl.when(s == pl.num_programs(1) - 1)
    def _():
        out = jnp.zeros((N_HEADS, HEAD_DIM), F32)
        for h in range(N_HEADS):
            out = jnp.where(head_of_row[:, :HEAD_DIM] == h, acc_ref[h], out)
        o_ref[0] = out


def _sb_decode(proj, bias_b, cache_k, cache_v, page_table, pages_per_step):
    nb, n_pages = page_table.shape
    n_pool, page = cache_k.shape[0], cache_k.shape[1]
    assert page == SB_BLOCK
    ck = cache_k.reshape(n_pool, page * N_HEADS, HEAD_DIM)
    cv = cache_v.reshape(n_pool, page * N_HEADS, HEAD_DIM)
    proj4 = proj.reshape(N_GROUPS, nb, N_HEADS, HEAD_DIM)
    uo = _later_sum_consts()
    pt = page_table.reshape(-1)

    def page_map(r):
        return lambda b, s, pt: (pt[b * n_pages + n_pages - 1 - (s * pages_per_step + r)], 0, 0)

    def page_specs():
        return [pl.BlockSpec((1, page * N_HEADS, HEAD_DIM), page_map(r)) for r in range(pages_per_step)]

    grid_spec = pltpu.PrefetchScalarGridSpec(
        num_scalar_prefetch=1,
        grid=(nb, n_pages // pages_per_step),
        in_specs=[
            pl.BlockSpec((1, 1, N_HEADS, HEAD_DIM), lambda b, s, pt: (G_SQ, b, 0, 0)),
            pl.BlockSpec(bias_b.shape, lambda b, s, pt: (0, 0)),
            pl.BlockSpec(uo.shape, lambda b, s, pt: (0, 0)),
        ] + page_specs() + page_specs(),
        out_specs=pl.BlockSpec((1, N_HEADS, HEAD_DIM), lambda b, s, pt: (b, 0, 0)),
        scratch_shapes=[pltpu.VMEM((N_HEADS, SB_BLOCK), F32),
                        pltpu.VMEM((N_HEADS, N_HEADS, HEAD_DIM), F32)],
    )
    out = pl.pallas_call(
        functools.partial(_sb_decode_kernel, pages_per_step=pages_per_step),
        grid_spec=grid_spec,
        out_shape=jax.ShapeDtypeStruct((nb, N_HEADS, HEAD_DIM), F32),
        compiler_params=pltpu.CompilerParams(dimension_semantics=("parallel", "arbitrary")),
        name="sb_decode",
    )(pt, proj4, bias_b, uo, *([ck] * pages_per_step), *([cv] * pages_per_step))
    return out.reshape(nb, D_MODEL)


def _merge_kernel(oa_ref, ob_ref, ga_ref, gb_ref, x_ref, wa_ref, wb_ref, wo_ref, gf_ref,
                  x1_ref, h2_ref):
    a = _dot(oa_ref[...].astype(BF16), wa_ref[...])
    b = _dot(ob_ref[...].astype(BF16), wb_ref[...])
    m = jax.nn.sigmoid(ga_ref[0]) * a + jax.nn.sigmoid(gb_ref[0]) * b
    x1 = x_ref[...] + _dot(m.astype(BF16), wo_ref[...])
    x1_ref[...] = x1
    h2_ref[...] = _rms(x1, gf_ref[...]).astype(BF16)


def _merge(o_a, o_b, proj, x, wa, wb, wo, g_ffn, tm):
    n = x.shape[0]
    row = pl.BlockSpec((tm, D_MODEL), lambda i: (i, 0))
    wspec = pl.BlockSpec((D_MODEL, D_MODEL), lambda i: (0, 0))
    return pl.pallas_call(
        _merge_kernel,
        grid=(n // tm,),
        in_specs=[
            row, row,
            pl.BlockSpec((1, tm, D_MODEL), lambda i: (G_GA, i, 0)),
            pl.BlockSpec((1, tm, D_MODEL), lambda i: (G_GB, i, 0)),
            row, wspec, wspec, wspec,
            pl.BlockSpec((1, D_MODEL), lambda i: (0, 0)),
        ],
        out_specs=[row, row],
        out_shape=[jax.ShapeDtypeStruct((n, D_MODEL), F32), jax.ShapeDtypeStruct((n, D_MODEL), BF16)],
        compiler_params=pltpu.CompilerParams(dimension_semantics=("parallel",)),
        name="merge",
    )(o_a, o_b, proj, proj, x, wa, wb, wo, g_ffn)


def _top_rows(vals, order, n):
    out_v, out_o = [], []
    for _ in range(n):
        m = jnp.max(vals, axis=0, keepdims=True)
        first = jnp.min(jnp.where(vals == m, order, 1e9), axis=0, keepdims=True)
        out_v.append(m)
        out_o.append(first)
        vals = jnp.where(order == first, NEG_INF, vals)
    return out_v, out_o


def _stack_rows(rows, sub):
    out = jnp.zeros(sub.shape, F32)
    for r, row in enumerate(rows):
        out = jnp.where(sub == r, row, out)
    return out


def _pair_candidates():
    pairs = [(0, b) for b in range(16)] + [(1, b) for b in range(8)]
    pairs += [(2, b) for b in range(5)] + [(4, b) for b in range(3)]
    pairs += [(3, b) for b in range(4)] + [(5, 0), (5, 1), (6, 0), (6, 1)]
    pairs += [(7, 0), (7, 1)] + [(a, 0) for a in range(8, 16)]
    assert sorted(pairs) == sorted((a, b) for a in range(16) for b in range(16) if (a + 1) * (b + 1) <= 16)
    return pairs + [(-1, 0)] * (-len(pairs) % 8)


def _peer_route_kernel(h_ref, wq_ref, sk_ref, order_ref, i1_ref, i2_ref, gate_ref):
    t = h_ref.shape[0]
    q = _dot(h_ref[...], wq_ref[...]).astype(BF16)
    key_iota = lax.broadcasted_iota(jnp.int32, (PEER_KEYS, t), 0).astype(F32)
    sub16 = lax.broadcasted_iota(jnp.int32, (PEER_TOPK, t), 0)
    sub8 = lax.broadcasted_iota(jnp.int32, (8, t), 0)
    pairs = _pair_candidates()
    order = order_ref[...]
    k1_rows, k2_rows, g_rows = [], [], []
    for h in range(N_HEADS):
        top_v, top_i = [], []
        for p in range(2):
            c0 = (h * 2 + p) * PEER_KEYS
            s = _dot_nt(sk_ref[p], q[:, c0:c0 + PEER_KEYS])
            v, i = _top_rows(s, key_iota, PEER_TOPK)
            top_v.append(v)
            top_i.append(i)
        cand = []
        for r0 in range(0, len(pairs), 8):
            tile = pairs[r0:r0 + 8]
            va = _stack_rows([top_v[0][max(a, 0)] for a, _ in tile], sub8)
            vb = _stack_rows([top_v[1][b] for _, b in tile], sub8)
            cand.append(va + vb)
        cand = jnp.where(order < 1e6, jnp.concatenate(cand, axis=0), NEG_INF)
        best, flat = _top_rows(cand, order, PEER_TOPK)
        flat = _stack_rows(flat, sub16)
        rank_a = jnp.floor(flat * (1.0 / PEER_TOPK))
        rank_b = flat - rank_a * float(PEER_TOPK)
        key1 = jnp.zeros_like(flat)
        key2 = jnp.zeros_like(flat)
        for r in range(PEER_TOPK):
            key1 = jnp.where(rank_a == float(r), top_i[0][r], key1)
            key2 = jnp.where(rank_b == float(r), top_i[1][r], key2)
        top = best[0]
        best = _stack_rows(best, sub16)
        ex = jnp.exp(best - top)
        g_rows.append(ex / jnp.sum(ex, axis=0, keepdims=True))
        k1_rows.append(key1)
        k2_rows.append(key2)
    i1_ref[...] = jnp.concatenate(k1_rows, axis=0).T
    i2_ref[...] = jnp.concatenate(k2_rows, axis=0).T
    gate_ref[...] = jnp.concatenate(g_rows, axis=0).T


def _peer_route(h2, wq, sub_keys, tt):
    n = h2.shape[0]
    sel = pl.BlockSpec((tt, PEER_SEL), lambda i: (i, 0))
    shp = jax.ShapeDtypeStruct((n, PEER_SEL), F32)
    flat = np.array([a * PEER_TOPK + b if a >= 0 else 1e9 for a, b in _pair_candidates()], np.float32)
    order = jnp.asarray(np.broadcast_to(flat[:, None], (flat.shape[0], tt)))
    return pl.pallas_call(
        _peer_route_kernel,
        grid=(n // tt,),
        in_specs=[
            pl.BlockSpec((tt, D_MODEL), lambda i: (i, 0)),
            pl.BlockSpec(wq.shape, lambda i: (0, 0)),
            pl.BlockSpec(sub_keys.shape, lambda i: (0, 0, 0)),
            pl.BlockSpec(order.shape, lambda i: (0, 0)),
        ],
        out_specs=[sel, sel, sel],
        out_shape=[shp, shp, shp],
        compiler_params=pltpu.CompilerParams(dimension_semantics=("parallel",)),
        name="peer_route",
    )(h2, wq, sub_keys, order)


def _slab_pitch(tt):
    p = tt // SUBLANES
    return SUBLANES * (p + 1 if p % 2 == 0 else p + 2)


def _peer_expert_kernel(h_ref, ut_ref, v_ref, i1_ref, i2_ref, gate_ref, x_ref, gf_ref,
                        y_ref, slab_ref, *, n_up, n_down, pitch, final_norm):
    s = pl.program_id(1)
    tt = h_ref.shape[0]
    slabs_up = ut_ref.shape[1] // PEER_KEYS
    slabs_down = v_ref.shape[0] // PEER_KEYS
    key_iota = lax.broadcasted_iota(jnp.int32, (PEER_KEYS, PEER_SEL), 0).astype(F32)

    def slab_rows(g):
        return pl.ds(pl.multiple_of(g * pitch, 8), tt)

    def token_rows(t):
        return pl.ds(t, PEER_KEYS, stride=pitch)

    @pl.when(s < n_up)
    def _():
        a = _dot(h_ref[...], ut_ref[...])
        for gl in range(slabs_up):
            slab_ref[slab_rows(s * slabs_up + gl), :] = a[:, gl * PEER_KEYS:(gl + 1) * PEER_KEYS]

    @pl.when(s == n_up - 1)
    def _():
        sub = lax.broadcasted_iota(jnp.int32, (PEER_GROUP, PEER_SEL), 0)

        def group_base(gi):
            base = gi * PEER_GROUP
            return base if isinstance(base, int) else pl.multiple_of(base, PEER_GROUP)

        def hits(t):
            return (key_iota == i1_ref[pl.ds(t, 1), :],
                    (key_iota == i2_ref[pl.ds(t, 1), :]).astype(BF16))

        def select(gi):
            base = group_base(gi)
            rows = []
            for u in range(PEER_GROUP):
                t = base + u
                hit1, hit2 = hits(t)
                pre = slab_ref[token_rows(t), :]
                col = jnp.broadcast_to(i2_ref[pl.ds(t, 1), :].astype(jnp.int32), (PEER_KEYS, PEER_SEL))
                picked = jnp.take_along_axis(pre, col, axis=1)
                rows.append(jnp.sum(jnp.where(hit1, picked, 0.0), axis=0, keepdims=True))
            return _stack_rows(rows, sub)

        def scatter(gi, pre):
            base = group_base(gi)
            wv = gate_ref[pl.ds(base, PEER_GROUP), :] * (0.5 * pre * (1.0 + lax.erf(pre * (0.5 ** 0.5))))
            for u in range(PEER_GROUP):
                hit1, hit2 = hits(base + u)
                w1 = jnp.where(hit1, wv[u:u + 1, :], 0.0).astype(BF16)
                slab_ref[token_rows(base + u), :] = _dot_nt(w1, hit2)

        def step(gi, pre_prev):
            pre = select(gi)
            scatter(gi - 1, pre_prev)
            return pre

        n_groups = tt // PEER_GROUP
        pre_last = lax.fori_loop(1, n_groups, step, select(0), unroll=PEER_GROUP_UNROLL)
        scatter(n_groups - 1, pre_last)
        y_ref[...] = jnp.zeros_like(y_ref)

    @pl.when(s >= n_up)
    def _():
        c = s - n_up
        w = jnp.concatenate(
            [slab_ref[slab_rows(c * slabs_down + gl), :].astype(BF16) for gl in range(slabs_down)], axis=1)
        y_ref[...] += _dot(w, v_ref[...])

    @pl.when(s == n_up + n_down - 1)
    def _():
        x2 = x_ref[...] + y_ref[...]
        y_ref[...] = _rms(x2, gf_ref[...]) if final_norm else x2


def _peer_experts(h2, u_t, v_tab, i1, i2, gates, x1, g_final, final_norm, tt, chunk_up, chunk_down):
    n = h2.shape[0]
    n_exp = v_tab.shape[0]
    n_up, n_down = n_exp // chunk_up, n_exp // chunk_down
    assert tt % PEER_GROUP == 0 and n % tt == 0
    pitch = _slab_pitch(tt)
    row = pl.BlockSpec((tt, D_MODEL), lambda i, s: (i, 0))
    sel = pl.BlockSpec((tt, PEER_SEL), lambda i, s: (i, 0))
    vmem = (PEER_KEYS * pitch * PEER_KEYS * 4 + 2 * (chunk_up + chunk_down) * D_MODEL * 2
            + 2 * tt * D_MODEL * (2 + 4 + 4) + 2 * 3 * tt * PEER_SEL * 4)
    vmem_limit = min(vmem + VMEM_HEADROOM_BYTES, V7X_VMEM_BYTES - VMEM_HEADROOM_BYTES)
    return pl.pallas_call(
        functools.partial(_peer_expert_kernel, n_up=n_up, n_down=n_down, pitch=pitch, final_norm=final_norm),
        grid=(n // tt, n_up + n_down),
        in_specs=[
            row,
            pl.BlockSpec((D_MODEL, chunk_up), lambda i, s: (0, jnp.minimum(s, n_up - 1))),
            pl.BlockSpec((chunk_down, D_MODEL), lambda i, s: (jnp.maximum(s - n_up, 0), 0)),
            sel, sel, sel, row,
            pl.BlockSpec((1, D_MODEL), lambda i, s: (0, 0)),
        ],
        out_specs=row,
        out_shape=jax.ShapeDtypeStruct((n, D_MODEL), F32),
        scratch_shapes=[pltpu.VMEM((PEER_KEYS * pitch, PEER_KEYS), F32)],
        compiler_params=pltpu.CompilerParams(dimension_semantics=("parallel", "arbitrary"),
                                             vmem_limit_bytes=vmem_limit),
        name="peer_experts",
    )(h2, u_t, v_tab, i1, i2, gates, x1, g_final)


def _tile(n, target):
    return min(n, target)


def kernel(x_prompt, x_sample, cache_k, cache_v, state_hgrn, page_table, g_attn, w_in, hg_lb_logits,
           hg_norm, sb_bias, w_branch_a, w_branch_b, w_out, g_ffn, peer_wq, peer_subkeys, peer_u, peer_v,
           g_final):
    depth = w_in.shape[0]
    batch, seq, _ = x_prompt.shape
    nb, dec_seq, _ = x_sample.shape
    assert dec_seq == 1
    n_p = batch * seq
    xp = x_prompt.reshape(n_p, D_MODEL)
    xs = x_sample.reshape(nb, D_MODEL)
    g_fin = g_final.reshape(1, D_MODEL)
    outs = {k: [] for k in ("sp", "kp", "vp", "ss", "ks", "vs")}
    for l in range(depth):
        w_in_l = w_in[l].astype(BF16)
        wa, wb, wo = (w[l].astype(BF16) for w in (w_branch_a, w_branch_b, w_out))
        wq = peer_wq[l].astype(BF16)
        sk = peer_subkeys[l].astype(BF16)
        u_t = peer_u[l].astype(BF16).T
        v_tab = peer_v[l].astype(BF16)
        g_a = g_attn[l].reshape(1, D_MODEL)
        g_f = g_ffn[l].reshape(1, D_MODEL)
        ng = hg_norm[l].reshape(1, D_MODEL)
        bias_b = jnp.broadcast_to(sb_bias[l].astype(F32)[:, None], (N_HEADS, SB_BLOCK))
        last = l == depth - 1

        proj, k_new, v_new = _norm_proj(xp, g_a, w_in_l, _tile(n_p, PROJ_ROWS))
        o_a, s_p = _hgrn_prompt(proj, hg_lb_logits, ng, batch, seq, l, _tile(seq, HG_STEP_TOKENS))
        o_b = _sb_prompt(proj, sb_bias[l].astype(F32), batch, seq, _tile(seq, SB_TILE))
        x1, h2 = _merge(o_a, o_b, proj, xp, wa, wb, wo, g_f, _tile(n_p, MERGE_ROWS))
        i1, i2, gates = _peer_route(h2, wq, sk, _tile(n_p, ROUTE_TOKENS))
        xp = _peer_experts(h2, u_t, v_tab, i1, i2, gates, x1, g_fin, last, _tile(n_p, EXPERT_TOKENS),
                           EXPERT_CHUNK_UP, EXPERT_CHUNK_DOWN)
        outs["sp"].append(s_p)
        outs["kp"].append(k_new.reshape(batch, seq, N_HEADS, HEAD_DIM))
        outs["vp"].append(v_new.reshape(batch, seq, N_HEADS, HEAD_DIM))

        proj_s, k_new_s, v_new_s = _norm_proj(xs, g_a, w_in_l, nb)
        o_a_s, s_s = _hgrn_step(proj_s, hg_lb_logits, hg_norm[l], state_hgrn[l], l)
        o_b_s = _sb_decode(proj_s, bias_b, cache_k[l], cache_v[l], page_table, DECODE_PAGES)
        x1_s, h2_s = _merge(o_a_s.reshape(nb, D_MODEL), o_b_s, proj_s, xs, wa, wb, wo, g_f, nb)
        pad = (-nb) % PEER_KEYS
        sel_s = _peer_route(jnp.pad(h2_s, ((0, pad), (0, 0))), wq, sk, PEER_KEYS)
        i1_s, i2_s, gates_s = (a[:nb] for a in sel_s)
        xs = _peer_experts(h2_s, u_t, v_tab, i1_s, i2_s, gates_s, x1_s, g_fin, last, nb,
                           EXPERT_CHUNK_DOWN, EXPERT_CHUNK_DOWN)
        outs["ss"].append(s_s)
        outs["ks"].append(k_new_s.reshape(nb, 1, N_HEADS, HEAD_DIM))
        outs["vs"].append(v_new_s.reshape(nb, 1, N_HEADS, HEAD_DIM))

    y_prompt = xp.reshape(batch, seq, D_MODEL)
    y_sample = xs.reshape(nb, 1, D_MODEL)
    return (y_prompt, y_sample, jnp.stack(outs["sp"]), jnp.stack(outs["kp"]), jnp.stack(outs["vp"]),
            jnp.stack(outs["ss"]), jnp.stack(outs["ks"]), jnp.stack(outs["vs"]))
```
